```python
import jax, jax.numpy as jnp
from jax import lax
import numpy as np

D_MODEL = 1024
BATCH = 8
SEQ = 4096
DEPTH = 1

EPS = 1e-6
HG_HEADS = 8
HG_DK = 128
HG_DV = 128
HG_KW = HG_HEADS * HG_DK
HG_WIDTH = HG_HEADS * HG_DV
HG_CHUNK = 64
M2_HEADDIM = 64
M2_WIDTH = D_MODEL
M2_HEADS = M2_WIDTH // M2_HEADDIM
M2_GROUPS = 2
M2_STATE = 128
M2_CONV = 4
M2_CHUNK = 128
M2_CONV_DIM = M2_WIDTH + 2 * M2_GROUPS * M2_STATE
MIX_WIDTH = HG_WIDTH + M2_WIDTH
IN_COLS = 2 * HG_KW + 2 * HG_WIDTH + M2_WIDTH + M2_CONV_DIM + M2_HEADS

kernel_name = "hymba_style_hgrn2_mamba2_hybrid"


def rmsnorm(x, w):
    xf = x.astype(jnp.float32)
    y = xf * lax.rsqrt(jnp.mean(xf * xf, axis=-1, keepdims=True) + EPS)
    return (y * w.astype(jnp.float32)).astype(x.dtype)


def hgrn2_mix(q_raw, f_raw, i_raw, g_raw, lb, norm_w):
    B_, L, _ = q_raw.shape
    C = HG_CHUNK
    n = L // C
    lbf = lb.astype(jnp.float32)
    fl = f_raw.astype(jnp.float32)
    q = jax.nn.silu(q_raw.astype(jnp.float32))
    f = lbf + (1.0 - lbf) * jax.nn.sigmoid(fl)
    k = (1.0 - lbf) * jax.nn.sigmoid(-fl)
    logf = jnp.log(f)
    v = i_raw.astype(jnp.float32)

    def to_chunks(t, d):
        return t.reshape(B_, n, C, HG_HEADS, d).transpose(1, 0, 3, 2, 4)

    mask = jnp.tril(jnp.ones((C, C), dtype=bool))

    def step(S, inp):
        qc, kc, vc, gc = inp
        b = jnp.cumsum(gc, axis=2)
        o_inter = jnp.einsum('bhtk,bhkv->bhtv', qc * jnp.exp(b), S)
        diff = b[:, :, :, None, :] - b[:, :, None, :, :]
        decay = jnp.exp(jnp.where(mask[:, :, None], diff, -jnp.inf))
        att = jnp.einsum('bhtk,bhsk,bhtsk->bhts', qc, kc, decay)
        o_intra = jnp.einsum('bhts,bhsv->bhtv', att, vc)
        b_last = b[:, :, -1:, :]
        S_new = jnp.exp(b_last[:, :, 0, :])[..., None] * S + jnp.einsum(
            'bhsk,bhsv->bhkv', kc * jnp.exp(b_last - b), vc)
        return S_new, o_inter + o_intra

    S0 = jnp.zeros((B_, HG_HEADS, HG_DK, HG_DV), jnp.float32)
    _, o = lax.scan(step, S0, (to_chunks(q, HG_DK), to_chunks(k, HG_DK),
                               to_chunks(v, HG_DV), to_chunks(logf, HG_DK)))
    o = o.transpose(1, 0, 3, 2, 4).reshape(B_, L, HG_HEADS, HG_DV)
    o = o * lax.rsqrt(jnp.mean(o * o, axis=-1, keepdims=True) + EPS)
    o = o * norm_w.astype(jnp.float32).reshape(HG_HEADS, HG_DV)
    o = o.reshape(B_, L, HG_WIDTH) * jax.nn.silu(g_raw.astype(jnp.float32))
    return o.astype(q_raw.dtype)


def causal_dwconv(u, w, b):
    ch = u.shape[-1]
    out = lax.conv_general_dilated(
        u, w[:, None, :].astype(u.dtype), window_strides=(1,),
        padding=[(M2_CONV - 1, 0)], dimension_numbers=('NWC', 'WIO', 'NWC'),
        feature_group_count=ch)
    return out + b.astype(u.dtype)


def ssd_scan(x, dt, A, Bm, Cm):
    B_, L, H, P = x.shape
    C = M2_CHUNK
    n = L // C
    G = M2_GROUPS
    hg = H // G
    N = M2_STATE
    xc = x.reshape(B_, n, C, G, hg, P).transpose(1, 0, 3, 4, 2, 5)
    dtc = dt.reshape(B_, n, C, G, hg).transpose(1, 0, 3, 4, 2)
    ac = dtc * A.reshape(G, hg)[None, None, :, :, None]
    Bc = Bm.reshape(B_, n, C, G, N).transpose(1, 0, 3, 2, 4)
    Cc = Cm.reshape(B_, n, C, G, N).transpose(1, 0, 3, 2, 4)
    mask = jnp.tril(jnp.ones((C, C), dtype=bool))

    def step(state, inp):
        xk, dtk, ak, bk, ck = inp
        cs = jnp.cumsum(ak, axis=-1)
        seg = cs[..., :, None] - cs[..., None, :]
        Lmat = jnp.exp(jnp.where(mask, seg, -jnp.inf))
        cb = jnp.einsum('bgtn,bgsn->bgts', ck, bk)
        w = cb[:, :, None] * Lmat * dtk[..., None, :]
        y_diag = jnp.einsum('bghts,bghsp->bghtp', w, xk)
        y_off = jnp.einsum('bgtn,bghpn->bghtp', ck, state) * jnp.exp(cs)[..., None]
        decay_s = jnp.exp(cs[..., -1:] - cs) * dtk
        state_new = jnp.exp(cs[..., -1])[..., None, None] * state + jnp.einsum(
            'bghs,bghsp,bgsn->bghpn', decay_s, xk, bk)
        return state_new, y_diag + y_off

    s0 = jnp.zeros((B_, G, hg, P, N), jnp.float32)
    _, y = lax.scan(step, s0, (xc, dtc, ac, Bc, Cc))
    return y.transpose(1, 0, 4, 2, 3, 5).reshape(B_, L, H, P)


def mamba2_mix(z, xbc, dt_raw, conv_w, conv_b, dt_bias, a_log, d_skip, norm_w):
    B_, L, _ = z.shape
    xbc = jax.nn.silu(causal_dwconv(xbc, conv_w, conv_b)).astype(jnp.float32)
    xs = xbc[..., :M2_WIDTH].reshape(B_, L, M2_HEADS, M2_HEADDIM)
    Bm = xbc[..., M2_WIDTH:M2_WIDTH + M2_GROUPS * M2_STATE].reshape(B_, L, M2_GROUPS, M2_STATE)
    Cm = xbc[..., M2_WIDTH + M2_GROUPS * M2_STATE:].reshape(B_, L, M2_GROUPS, M2_STATE)
    dt = jax.nn.softplus(dt_raw.astype(jnp.float32) + dt_bias.astype(jnp.float32))
    A = -jnp.exp(a_log.astype(jnp.float32))
    y = ssd_scan(xs, dt, A, Bm, Cm) + d_skip.astype(jnp.float32)[:, None] * xs
    y = y.reshape(B_, L, M2_WIDTH) * jax.nn.silu(z.astype(jnp.float32))
    y = y.reshape(B_, L, M2_GROUPS, M2_WIDTH // M2_GROUPS)
    y = y * lax.rsqrt(jnp.mean(y * y, axis=-1, keepdims=True) + EPS)
    y = y.reshape(B_, L, M2_WIDTH) * norm_w.astype(jnp.float32)
    return y.astype(z.dtype)


def setup_inputs(seed: int = 0) -> dict:
    key = jax.random.key(seed)
    ks = jax.random.split(key, 16)
    f32 = jnp.float32
    x = jax.random.normal(ks[0], (BATCH, SEQ, D_MODEL), f32)
    norm_w = 1.0 + 0.02 * jax.random.normal(ks[1], (DEPTH, D_MODEL), f32)
    w_in = jax.random.normal(ks[2], (DEPTH, D_MODEL, IN_COLS), f32) * D_MODEL ** -0.5
    hg_lb_logits = 0.1 * jax.random.normal(ks[3], (DEPTH + 1, HG_KW), f32)
    hg_norm_w = 1.0 + 0.02 * jax.random.normal(ks[4], (DEPTH, HG_WIDTH), f32)
    m2_conv_w = jax.random.uniform(ks[5], (DEPTH, M2_CONV, M2_CONV_DIM), f32, -1.0, 1.0) * M2_CONV ** -0.5
    m2_conv_b = 0.02 * jax.random.normal(ks[6], (DEPTH, M2_CONV_DIM), f32)
    dt0 = jnp.exp(jax.random.uniform(ks[7], (DEPTH, M2_HEADS), f32, np.log(1e-3), np.log(1e-1)))
    m2_dt_bias = dt0 + jnp.log(-jnp.expm1(-dt0))
    m2_a_log = jnp.log(jax.random.uniform(ks[8], (DEPTH, M2_HEADS), f32, 1.0, 16.0))
    m2_d_skip = 1.0 + 0.02 * jax.random.normal(ks[9], (DEPTH, M2_HEADS), f32)
    m2_norm_w = 1.0 + 0.02 * jax.random.normal(ks[10], (DEPTH, M2_WIDTH), f32)
    w_out = jax.random.normal(ks[11], (DEPTH, MIX_WIDTH, D_MODEL), f32) * MIX_WIDTH ** -0.5
    final_norm_w = 1.0 + 0.02 * jax.random.normal(ks[12], (D_MODEL,), f32)
    return {"x": x, "norm_w": norm_w, "w_in": w_in, "hg_lb_logits": hg_lb_logits,
            "hg_norm_w": hg_norm_w, "m2_conv_w": m2_conv_w, "m2_conv_b": m2_conv_b,
            "m2_dt_bias": m2_dt_bias, "m2_a_log": m2_a_log, "m2_d_skip": m2_d_skip,
            "m2_norm_w": m2_norm_w, "w_out": w_out, "final_norm_w": final_norm_w}


def reference(x, norm_w, w_in, hg_lb_logits, hg_norm_w, m2_conv_w, m2_conv_b,
              m2_dt_bias, m2_a_log, m2_d_skip, m2_norm_w, w_out, final_norm_w):
    lb_all = jnp.cumsum(jax.nn.softmax(hg_lb_logits.astype(jnp.float32), axis=0), axis=0)
    sizes = [HG_KW, HG_KW, HG_WIDTH, HG_WIDTH, M2_WIDTH, M2_CONV_DIM, M2_HEADS]
    split_idx = [int(s) for s in np.cumsum(sizes)[:-1]]
    for l in range(DEPTH):
        h = rmsnorm(x, norm_w[l])
        proj = jnp.einsum('bsd,de->bse', h, w_in[l])
        q_raw, f_raw, i_raw, g_raw, z, xbc, dt_raw = jnp.split(proj, split_idx, axis=-1)
        hg_out = hgrn2_mix(q_raw, f_raw, i_raw, g_raw, lb_all[l], hg_norm_w[l])
        m2_out = mamba2_mix(z, xbc, dt_raw, m2_conv_w[l], m2_conv_b[l], m2_dt_bias[l],
                            m2_a_log[l], m2_d_skip[l], m2_norm_w[l])
        mix = jnp.concatenate([hg_out, m2_out], axis=-1)
        x = x + jnp.einsum('bse,ed->bsd', mix, w_out[l]).astype(x.dtype)
    return rmsnorm(x, final_norm_w)
```

```python
import functools

import jax
import jax.numpy as jnp
from jax import lax
from jax.experimental import pallas as pl
from jax.experimental.pallas import tpu as pltpu

F32 = jnp.float32
BF16 = jnp.bfloat16

EPS = 1e-6
D_MODEL = 1024
HG_HEADS = 8
HG_DK = 128
HG_DV = 128
HG_W = HG_HEADS * HG_DK
HG_CHUNK = 64
M2_HEADDIM = 64
M2_W = 1024
M2_HEADS = M2_W // M2_HEADDIM
M2_GROUPS = 2
M2_STATE = 128
M2_CONV = 4
M2_CHUNK = 128
M2_CONV_DIM = M2_W + 2 * M2_GROUPS * M2_STATE
M2_GW = M2_W // M2_GROUPS
M2_HPG = M2_HEADS // M2_GROUPS

OFF_Q = 0
OFF_F = OFF_Q + HG_W
OFF_I = OFF_F + HG_W
OFF_G = OFF_I + HG_W
OFF_Z = OFF_G + HG_W
OFF_XBC = OFF_Z + M2_W
OFF_DT = OFF_XBC + M2_CONV_DIM
MAIN_COLS = OFF_DT

SEQ_TILE = 256
CONV_PAD = 8
VMEM_LIMIT_BYTES = 58 * 1024 * 1024


def _dot(a, b):
    return jnp.dot(a, b, preferred_element_type=F32)


def _dot_nt(a, b):
    return lax.dot_general(a, b, (((1,), (1,)), ((), ())), preferred_element_type=F32)


def _dot_tn(a, b):
    return lax.dot_general(a, b, (((0,), (0,)), ((), ())), preferred_element_type=F32)


def _split3(x):
    hi = x.astype(BF16)
    r1 = x - hi.astype(F32)
    mid = r1.astype(BF16)
    lo = (r1 - mid.astype(F32)).astype(BF16)
    return hi, mid, lo


def _dot_exact_left(mat_bf16, x):
    hi, mid, lo = _split3(x)
    return (_dot(mat_bf16, lo) + _dot(mat_bf16, mid)) + _dot(mat_bf16, hi)


def _dot_exact_right(x, mat_bf16):
    hi, mid, lo = _split3(x)
    return (_dot(lo, mat_bf16) + _dot(mid, mat_bf16)) + _dot(hi, mat_bf16)


def _sigmoid(x):
    return 1.0 / (1.0 + jnp.exp(-x))


def _silu(x):
    return x * _sigmoid(x)


def _softplus(x):
    return jnp.maximum(x, 0.0) + jnp.log(1.0 + jnp.exp(-jnp.abs(x)))


def _block_tril(n, block, upper=False):
    r = lax.broadcasted_iota(jnp.int32, (n, n), 0)
    c = lax.broadcasted_iota(jnp.int32, (n, n), 1)
    same = (r // block) == (c // block)
    tri = (r <= c) if upper else (c <= r)
    return jnp.where(same & tri, 1.0, 0.0).astype(BF16)


def _layer_kernel(x_ref, norm_w_ref, w_main_ref, wdt_hi_ref, wdt_lo_ref, wdtT_hi_ref, wdtT_lo_ref,
                  lb_logits_ref, hg_norm_w_ref, conv_w_ref, conv_b_ref,
                  dt_bias_row_ref, dt_bias_col_ref, a_log_row_ref, a_log_col_ref,
                  d_skip_ref, m2_norm_w_ref, w_out_ref, final_norm_w_ref,
                  o_ref,
                  hg_state_ref, m2_state_ref, ubuf_ref, o_hg_ref, y_m2_ref,
                  *, layer, apply_final_norm):
    tl = x_ref.shape[1]
    lt = pl.program_id(1)

    @pl.when(lt == 0)
    def _():
        hg_state_ref[...] = jnp.zeros_like(hg_state_ref)
        m2_state_ref[...] = jnp.zeros_like(m2_state_ref)
        ubuf_ref[0:CONV_PAD, :] = jnp.zeros((CONV_PAD, M2_CONV_DIM), F32)

    x = x_ref[0]
    ms = jnp.mean(x * x, axis=-1, keepdims=True)
    h = (x * lax.rsqrt(ms + EPS)) * norm_w_ref[...]
    h_bf = h.astype(BF16)
    h_lo = (h - h_bf.astype(F32)).astype(BF16)

    def proj(off, width):
        return _dot(h_bf, w_main_ref[:, off:off + width])

    logits = lb_logits_ref[...]
    mx = jnp.max(logits, axis=0, keepdims=True)
    ez = jnp.exp(logits - mx)
    lb = jnp.sum(ez[0:layer + 1], axis=0, keepdims=True) / jnp.sum(ez, axis=0, keepdims=True)
    one_m_lb = 1.0 - lb

    fl = proj(OFF_F, HG_W)
    sg = _sigmoid(fl)
    f = lb + one_m_lb * sg
    k = one_m_lb * (1.0 - sg)
    logf = jnp.log(f)
    b = _dot_exact_left(_block_tril(tl, HG_CHUNK), logf)
    q = _silu(proj(OFF_Q, HG_W))
    qt = (q * jnp.exp(b)).astype(BF16)
    kt = k * jnp.exp(-b)
    kt_bf = kt.astype(BF16)
    v_bf = proj(OFF_I, HG_W).astype(BF16)

    rr = lax.broadcasted_iota(jnp.int32, (HG_CHUNK, HG_CHUNK), 0)
    cc = lax.broadcasted_iota(jnp.int32, (HG_CHUNK, HG_CHUNK), 1)
    causal_hg = cc <= rr

    for hd in range(HG_HEADS):
        cols = slice(hd * HG_DK, (hd + 1) * HG_DK)
        st = hg_state_ref[hd]
        for c in range(tl // HG_CHUNK):
            rows = slice(c * HG_CHUNK, (c + 1) * HG_CHUNK)
            qc = qt[rows, cols]
            o_inter = _dot_nt(qc, st.astype(BF16))
            att = jnp.where(causal_hg, _dot_nt(qc, kt_bf[rows, cols]), 0.0)
            vc = v_bf[rows, cols]
            o = o_inter + _dot(att.astype(BF16), vc)
            e_last = jnp.exp(b[(c + 1) * HG_CHUNK - 1:(c + 1) * HG_CHUNK, cols])
            khat = (kt[rows, cols] * e_last).astype(BF16)
            st = st * e_last + _dot_tn(vc, khat)
            rs = lax.rsqrt(jnp.mean(o * o, axis=-1, keepdims=True) + EPS)
            o_hg_ref[rows, cols] = o * rs
        hg_state_ref[hd] = st

    g = _silu(proj(OFF_G, HG_W))
    mix_hg = (o_hg_ref[...] * hg_norm_w_ref[...] * g).astype(BF16)

    u = proj(OFF_XBC, M2_CONV_DIM)
    ubuf_ref[CONV_PAD:CONV_PAD + tl, :] = u
    conv = u * conv_w_ref[M2_CONV - 1:M2_CONV, :]
    for j in range(M2_CONV - 1):
        sh = M2_CONV - 1 - j
        conv = conv + ubuf_ref[CONV_PAD - sh:CONV_PAD - sh + tl, :] * conv_w_ref[j:j + 1, :]
    ubuf_ref[0:CONV_PAD, :] = ubuf_ref[tl:tl + CONV_PAD, :]
    xbc = _silu(conv + conv_b_ref[...])
    xs = xbc[:, 0:M2_W]
    bm_bf = xbc[:, M2_W:M2_W + M2_GROUPS * M2_STATE].astype(BF16)
    cm_bf = xbc[:, M2_W + M2_GROUPS * M2_STATE:].astype(BF16)

    wdt_hi = wdt_hi_ref[...]
    wdt_lo = wdt_lo_ref[...]
    dt_raw = (_dot(h_lo, wdt_hi) + _dot(h_bf, wdt_lo)) + _dot(h_bf, wdt_hi)
    wdtT_hi = wdtT_hi_ref[...]
    wdtT_lo = wdtT_lo_ref[...]
    dt_rawT = (_dot_nt(wdtT_hi, h_lo) + _dot_nt(wdtT_lo, h_bf)) + _dot_nt(wdtT_hi, h_bf)
    dt = _softplus(dt_raw + dt_bias_row_ref[...])
    dtT = _softplus(dt_rawT + dt_bias_col_ref[...])
    a = dt * (-jnp.exp(a_log_row_ref[...]))
    aT = dtT * (-jnp.exp(a_log_col_ref[...]))
    cs = _dot_exact_left(_block_tril(tl, M2_CHUNK), a)
    csT = _dot_exact_right(aT, _block_tril(tl, M2_CHUNK, upper=True))

    hr = lax.broadcasted_iota(jnp.int32, (M2_HEADS, M2_W), 0)
    hc = lax.broadcasted_iota(jnp.int32, (M2_HEADS, M2_W), 1)
    expand = jnp.where(hc // M2_HEADDIM == hr, 1.0, 0.0).astype(BF16)

    r2 = lax.broadcasted_iota(jnp.int32, (M2_CHUNK, M2_CHUNK), 0)
    c2 = lax.broadcasted_iota(jnp.int32, (M2_CHUNK, M2_CHUNK), 1)
    causal_m2 = c2 <= r2

    for c in range(tl // M2_CHUNK):
        rows = slice(c * M2_CHUNK, (c + 1) * M2_CHUNK)
        cs_c = cs[rows, :]
        cs_last = cs_c[M2_CHUNK - 1:M2_CHUNK, :]
        ecs_x = _dot(jnp.exp(cs_c).astype(BF16), expand)
        dec_x = _dot((jnp.exp(cs_last - cs_c) * dt[rows, :]).astype(BF16), expand)
        xs_c = xs[rows, :]
        xs_bf = xs_c.astype(BF16)
        xdec_bf = (xs_c * dec_x).astype(BF16)
        for gi in range(M2_GROUPS):
            gcols = slice(gi * M2_GW, (gi + 1) * M2_GW)
            ncols = slice(gi * M2_STATE, (gi + 1) * M2_STATE)
            bg = bm_bf[rows, ncols]
            cg = cm_bf[rows, ncols]
            cb = _dot_nt(cg, bg)
            stT = m2_state_ref[gi]
            y_off = _dot(cg, stT.astype(BF16)) * ecs_x[:, gcols]
            for hh in range(M2_HPG):
                hidx = gi * M2_HPG + hh
                seg = cs_c[:, hidx:hidx + 1] - csT[hidx:hidx + 1, rows]
                lm = jnp.exp(jnp.where(causal_m2, seg, -jnp.inf))
                w = (cb * lm) * dtT[hidx:hidx + 1, rows]
                pc = slice(hidx * M2_HEADDIM, (hidx + 1) * M2_HEADDIM)
                y_diag = _dot(w.astype(BF16), xs_bf[:, pc])
                y_m2_ref[rows, pc] = y_diag + y_off[:, hh * M2_HEADDIM:(hh + 1) * M2_HEADDIM]
            e_last_x = ecs_x[M2_CHUNK - 1:M2_CHUNK, gcols]
            m2_state_ref[gi] = stT * e_last_x + _dot_tn(bg, xdec_bf[:, gcols])

    y = y_m2_ref[...] + d_skip_ref[...] * xs
    y = y * _silu(proj(OFF_Z, M2_W))
    parts = []
    for gi in range(M2_GROUPS):
        yg = y[:, gi * M2_GW:(gi + 1) * M2_GW]
        rs = lax.rsqrt(jnp.mean(yg * yg, axis=-1, keepdims=True) + EPS)
        parts.append(yg * rs)
    mix_m2 = (jnp.concatenate(parts, axis=-1) * m2_norm_w_ref[...]).astype(BF16)

    out = _dot(mix_hg, w_out_ref[0:HG_W, :]) + _dot(mix_m2, w_out_ref[HG_W:HG_W + M2_W, :])
    xr = x + out
    if apply_final_norm:
        ms2 = jnp.mean(xr * xr, axis=-1, keepdims=True)
        xr = (xr * lax.rsqrt(ms2 + EPS)) * final_norm_w_ref[...]
    o_ref[0] = xr.astype(o_ref.dtype)


def _const_spec(shape):
    nd = len(shape)
    return pl.BlockSpec(shape, lambda b, l: (0,) * nd, pipeline_mode=pl.Buffered(1))


def _layer(x, layer, depth, norm_w, w_in, hg_lb_logits, hg_norm_w, conv_w, conv_b, dt_bias, a_log,
           d_skip, m2_norm_w, w_out, final_norm_w):
    bsz, seq, d = x.shape
    assert d == D_MODEL and seq % SEQ_TILE == 0
    assert w_in.shape == (D_MODEL, MAIN_COLS + M2_HEADS)
    tl = SEQ_TILE

    w_main = w_in[:, :MAIN_COLS].astype(BF16)
    wdt = w_in[:, MAIN_COLS:]
    wdt_hi = wdt.astype(BF16)
    wdt_lo = (wdt - wdt_hi.astype(F32)).astype(BF16)
    row = lambda t: t.reshape(1, -1).astype(F32)
    col = lambda t: t.reshape(-1, 1).astype(F32)

    operands = (
        x, row(norm_w), w_main, wdt_hi, wdt_lo, wdt_hi.T, wdt_lo.T,
        hg_lb_logits.astype(F32), row(hg_norm_w), conv_w.astype(F32), row(conv_b),
        row(dt_bias), col(dt_bias), row(a_log), col(a_log),
        row(jnp.repeat(d_skip, M2_HEADDIM)), row(m2_norm_w), w_out.astype(BF16), row(final_norm_w),
    )
    in_specs = [pl.BlockSpec((1, tl, d), lambda b, l: (b, l, 0))]
    in_specs += [_const_spec(op.shape) for op in operands[1:]]

    kern = functools.partial(_layer_kernel, layer=layer, apply_final_norm=(layer == depth - 1))
    return pl.pallas_call(
        kern,
        grid=(bsz, seq // tl),
        in_specs=in_specs,
        out_specs=pl.BlockSpec((1, tl, d), lambda b, l: (b, l, 0)),
        out_shape=jax.ShapeDtypeStruct(x.shape, x.dtype),
        scratch_shapes=[
            pltpu.VMEM((HG_HEADS, HG_DV, HG_DK), F32),
            pltpu.VMEM((M2_GROUPS, M2_STATE, M2_GW), F32),
            pltpu.VMEM((CONV_PAD + tl, M2_CONV_DIM), F32),
            pltpu.VMEM((tl, HG_W), F32),
            pltpu.VMEM((tl, M2_W), F32),
        ],
        compiler_params=pltpu.CompilerParams(
            dimension_semantics=("arbitrary", "arbitrary"),
            vmem_limit_bytes=VMEM_LIMIT_BYTES,
        ),
        name="hybrid_layer",
    )(*operands)


@jax.jit
def kernel(x, norm_w, w_in, hg_lb_logits, hg_norm_w, m2_conv_w, m2_conv_b, m2_dt_bias, m2_a_log,
           m2_d_skip, m2_norm_w, w_out, final_norm_w):
    depth = w_in.shape[0]
    for l in range(depth):
        x = _layer(x, l, depth, norm_w[l], w_in[l], hg_lb_logits, hg_norm_w[l], m2_conv_w[l],
                   m2_conv_b[l], m2_dt_bias[l], m2_a_log[l], m2_d_skip[l], m2_norm_w[l], w_out[l],
                   final_norm_w)
    return x
```

```python
import functools

import jax
import jax.numpy as jnp
from jax import lax
from jax.experimental import pallas as pl
from jax.experimental.pallas import tpu as pltpu

F32 = jnp.float32
BF16 = jnp.bfloat16

EPS = 1e-6
D_MODEL = 1024
HG_HEADS = 8
HG_DK = 128
HG_DV = 128
HG_W = HG_HEADS * HG_DK
HG_CHUNK = 64
M2_HEADDIM = 64
M2_W = 1024
M2_HEADS = M2_W // M2_HEADDIM
M2_GROUPS = 2
M2_STATE = 128
M2_CONV = 4
M2_CHUNK = 128
M2_CONV_DIM = M2_W + 2 * M2_GROUPS * M2_STATE
M2_GW = M2_W // M2_GROUPS
M2_HPG = M2_HEADS // M2_GROUPS
M2_QUAD = 4

OFF_Q = 0
OFF_F = OFF_Q + HG_W
OFF_I = OFF_F + HG_W
OFF_G = OFF_I + HG_W
OFF_Z = OFF_G + HG_W
OFF_XBC = OFF_Z + M2_W
OFF_DT = OFF_XBC + M2_CONV_DIM
MAIN_COLS = OFF_DT

SEQ_TILE = 256
CONV_PAD = 8
VMEM_LIMIT_BYTES = 58 * 1024 * 1024


def _dot(a, b):
    return jnp.dot(a, b, preferred_element_type=F32)


def _dot_nt(a, b):
    return lax.dot_general(a, b, (((1,), (1,)), ((), ())), preferred_element_type=F32)


def _dot_tn(a, b):
    return lax.dot_general(a, b, (((0,), (0,)), ((), ())), preferred_element_type=F32)


def _split3(x):
    hi = x.astype(BF16)
    r1 = x - hi.astype(F32)
    mid = r1.astype(BF16)
    lo = (r1 - mid.astype(F32)).astype(BF16)
    return hi, mid, lo


def _dot_exact_left(mat_bf16, x):
    hi, mid, lo = _split3(x)
    return (_dot(mat_bf16, lo) + _dot(mat_bf16, mid)) + _dot(mat_bf16, hi)


def _dot_exact_right(x, mat_bf16):
    hi, mid, lo = _split3(x)
    return (_dot(lo, mat_bf16) + _dot(mid, mat_bf16)) + _dot(hi, mat_bf16)


def _sigmoid(x):
    return 1.0 / (1.0 + jnp.exp(-x))


def _silu(x):
    return x * _sigmoid(x)


def _softplus(x):
    return jnp.maximum(x, 0.0) + jnp.log(1.0 + jnp.exp(-jnp.abs(x)))


def _block_tril(n, block, upper=False):
    r = lax.broadcasted_iota(jnp.int32, (n, n), 0)
    c = lax.broadcasted_iota(jnp.int32, (n, n), 1)
    same = (r // block) == (c // block)
    tri = (r <= c) if upper else (c <= r)
    return jnp.where(same & tri, 1.0, 0.0).astype(BF16)


def _layer_kernel(x_ref, norm_w_ref, w_main_ref, wdtT_hi_ref, wdtT_lo_ref,
                  lb_logits_ref, hg_norm_w_ref, conv_w_ref, conv_b_ref,
                  dt_bias_col_ref, a_log_col_ref,
                  d_skip_ref, m2_norm_w_ref, w_out_ref, final_norm_w_ref,
                  o_ref,
                  hg_state_ref, m2_state_ref, ubuf_ref, o_hg_ref, y_m2_ref,
                  *, layer, apply_final_norm):
    tl = x_ref.shape[1]
    lt = pl.program_id(1)

    @pl.when(lt == 0)
    def _():
        hg_state_ref[...] = jnp.zeros_like(hg_state_ref)
        m2_state_ref[...] = jnp.zeros_like(m2_state_ref)
        ubuf_ref[0:CONV_PAD, :] = jnp.zeros((CONV_PAD, M2_CONV_DIM), F32)

    x = x_ref[0]
    ms = jnp.mean(x * x, axis=-1, keepdims=True)
    h = (x * lax.rsqrt(ms + EPS)) * norm_w_ref[...]
    h_bf = h.astype(BF16)
    h_lo = (h - h_bf.astype(F32)).astype(BF16)

    def proj(off, width):
        return _dot(h_bf, w_main_ref[:, off:off + width])

    u = proj(OFF_XBC, M2_CONV_DIM)
    fl = proj(OFF_F, HG_W)
    q_raw = proj(OFF_Q, HG_W)
    wdtT_hi = wdtT_hi_ref[...]
    wdtT_lo = wdtT_lo_ref[...]
    dt_rawT = (_dot_nt(wdtT_hi, h_lo) + _dot_nt(wdtT_lo, h_bf)) + _dot_nt(wdtT_hi, h_bf)
    v_bf = proj(OFF_I, HG_W).astype(BF16)
    g_raw = proj(OFF_G, HG_W)
    z_raw = proj(OFF_Z, M2_W)

    logits = lb_logits_ref[...]
    mx = jnp.max(logits, axis=0, keepdims=True)
    ez = jnp.exp(logits - mx)
    lb = jnp.sum(ez[0:layer + 1], axis=0, keepdims=True) / jnp.sum(ez, axis=0, keepdims=True)
    one_m_lb = 1.0 - lb

    sg = _sigmoid(fl)
    f = lb + one_m_lb * sg
    k = one_m_lb * (1.0 - sg)
    logf = jnp.log(f)
    b = _dot_exact_left(_block_tril(tl, HG_CHUNK), logf)
    q = _silu(q_raw)
    qt = q * jnp.exp(b)
    kt = k * jnp.exp(-b)
    qt_bf = qt.astype(BF16)
    kt_bf = kt.astype(BF16)

    nch = tl // HG_CHUNK
    crow = lambda c: slice(c * HG_CHUNK, (c + 1) * HG_CHUNK)
    tot = [b[(c + 1) * HG_CHUNK - 1:(c + 1) * HG_CHUNK, :] for c in range(nch)]
    start = [jnp.zeros_like(tot[0])]
    for c in range(nch):
        start.append(start[c] + tot[c])
    e_tile = jnp.exp(start[nch])
    zeros_blk = jnp.zeros((HG_CHUNK, HG_W), BF16)
    qg_blocks, kend_blocks = [], []
    qsrc_blocks = [[] for _ in range(nch - 1)]
    ksrc_blocks = [[] for _ in range(nch - 1)]
    for c in range(nch):
        qt_c = qt[crow(c), :]
        khat_c = kt[crow(c), :] * jnp.exp(tot[c])
        qg_blocks.append((qt_c * jnp.exp(start[c])).astype(BF16))
        kend_blocks.append((khat_c * jnp.exp(start[nch] - start[c + 1])).astype(BF16))
        for j in range(nch - 1):
            if c <= j:
                qsrc_blocks[j].append(zeros_blk)
            elif c == j + 1:
                qsrc_blocks[j].append(qt_bf[crow(c), :])
            else:
                qsrc_blocks[j].append((qt_c * jnp.exp(start[c] - start[j + 1])).astype(BF16))
            ksrc_blocks[j].append(khat_c.astype(BF16) if c == j else zeros_blk)
    qg_bf = jnp.concatenate(qg_blocks, axis=0)
    kend_bf = jnp.concatenate(kend_blocks, axis=0)
    qsrc = [jnp.concatenate(blks, axis=0) for blks in qsrc_blocks]
    ksrc = [jnp.concatenate(blks, axis=0) for blks in ksrc_blocks]

    rr = lax.broadcasted_iota(jnp.int32, (tl, tl), 0)
    cc = lax.broadcasted_iota(jnp.int32, (tl, tl), 1)
    diag_causal = (cc <= rr) & ((rr // HG_CHUNK) == (cc // HG_CHUNK))

    for hd in range(HG_HEADS):
        cols = slice(hd * HG_DK, (hd + 1) * HG_DK)
        st = hg_state_ref[hd]
        att = jnp.where(diag_causal, _dot_nt(qt_bf[:, cols], kt_bf[:, cols]), 0.0)
        att = att + _dot_nt(jnp.concatenate([qj[:, cols] for qj in qsrc], axis=1),
                            jnp.concatenate([kj[:, cols] for kj in ksrc], axis=1))
        vh = v_bf[:, cols]
        o = _dot_nt(qg_bf[:, cols], st.astype(BF16)) + _dot(att.astype(BF16), vh)
        hg_state_ref[hd] = st * e_tile[:, cols] + _dot_tn(vh, kend_bf[:, cols])
        rs = lax.rsqrt(jnp.mean(o * o, axis=-1, keepdims=True) + EPS)
        o_hg_ref[:, cols] = o * rs

    g = _silu(g_raw)
    mix_hg = (o_hg_ref[...] * hg_norm_w_ref[...] * g).astype(BF16)

    ubuf_ref[CONV_PAD:CONV_PAD + tl, :] = u
    conv = u * conv_w_ref[M2_CONV - 1:M2_CONV, :]
    for j in range(M2_CONV - 1):
        sh = M2_CONV - 1 - j
        conv = conv + ubuf_ref[CONV_PAD - sh:CONV_PAD - sh + tl, :] * conv_w_ref[j:j + 1, :]
    ubuf_ref[0:CONV_PAD, :] = ubuf_ref[tl:tl + CONV_PAD, :]
    xbc = _silu(conv + conv_b_ref[...])
    xs = xbc[:, 0:M2_W]
    bm_bf = xbc[:, M2_W:M2_W + M2_GROUPS * M2_STATE].astype(BF16)
    cm_bf = xbc[:, M2_W + M2_GROUPS * M2_STATE:].astype(BF16)

    dtT = _softplus(dt_rawT + dt_bias_col_ref[...])
    aT = dtT * (-jnp.exp(a_log_col_ref[...]))
    csT = _dot_exact_right(aT, _block_tril(tl, M2_CHUNK, upper=True))
    dt = dtT.T
    cs = csT.T

    hr = lax.broadcasted_iota(jnp.int32, (M2_HEADS, M2_W), 0)
    hc = lax.broadcasted_iota(jnp.int32, (M2_HEADS, M2_W), 1)
    expand = jnp.where(hc // M2_HEADDIM == hr, 1.0, 0.0).astype(BF16)

    r2 = lax.broadcasted_iota(jnp.int32, (M2_CHUNK, M2_CHUNK), 0)
    c2 = lax.broadcasted_iota(jnp.int32, (M2_CHUNK, M2_CHUNK), 1)
    causal_m2 = c2 <= r2
    qr = lax.broadcasted_iota(jnp.int32, (M2_QUAD * M2_CHUNK, M2_QUAD * M2_HEADDIM), 0)
    qc = lax.broadcasted_iota(jnp.int32, (M2_QUAD * M2_CHUNK, M2_QUAD * M2_HEADDIM), 1)
    quad_mask = (qr // M2_CHUNK) == (qc // M2_HEADDIM)

    for c in range(tl // M2_CHUNK):
        rows = slice(c * M2_CHUNK, (c + 1) * M2_CHUNK)
        cs_c = cs[rows, :]
        cs_last = cs_c[M2_CHUNK - 1:M2_CHUNK, :]
        ecs_x = _dot(jnp.exp(cs_c).astype(BF16), expand)
        dec_x = _dot((jnp.exp(cs_last - cs_c) * dt[rows, :]).astype(BF16), expand)
        xs_c = xs[rows, :]
        xs_bf = xs_c.astype(BF16)
        xdec_bf = (xs_c * dec_x).astype(BF16)
        for gi in range(M2_GROUPS):
            gcols = slice(gi * M2_GW, (gi + 1) * M2_GW)
            ncols = slice(gi * M2_STATE, (gi + 1) * M2_STATE)
            bg = bm_bf[rows, ncols]
            cg = cm_bf[rows, ncols]
            cb = _dot_nt(cg, bg)
            stT = m2_state_ref[gi]
            y_off = _dot(cg, stT.astype(BF16)) * ecs_x[:, gcols]
            for qd in range(M2_HPG // M2_QUAD):
                ws = []
                for hh in range(M2_QUAD):
                    hidx = gi * M2_HPG + qd * M2_QUAD + hh
                    seg = cs_c[:, hidx:hidx + 1] - csT[hidx:hidx + 1, rows]
                    lm = jnp.exp(jnp.where(causal_m2, seg, -jnp.inf))
                    ws.append(((cb * lm) * dtT[hidx:hidx + 1, rows]).astype(BF16))
                q0 = (gi * M2_HPG + qd * M2_QUAD) * M2_HEADDIM
                xq = xs_bf[:, q0:q0 + M2_QUAD * M2_HEADDIM]
                x_bd = jnp.where(quad_mask, jnp.concatenate([xq] * M2_QUAD, axis=0), jnp.zeros_like(quad_mask, BF16))
                y_diag = _dot(jnp.concatenate(ws, axis=1), x_bd)
                o0 = qd * M2_QUAD * M2_HEADDIM
                y_m2_ref[rows, q0:q0 + M2_QUAD * M2_HEADDIM] = y_diag + y_off[:, o0:o0 + M2_QUAD * M2_HEADDIM]
            e_last_x = ecs_x[M2_CHUNK - 1:M2_CHUNK, gcols]
            m2_state_ref[gi] = stT * e_last_x + _dot_tn(bg, xdec_bf[:, gcols])

    y = y_m2_ref[...] + d_skip_ref[...] * xs
    y = y * _silu(z_raw)
    parts = []
    for gi in range(M2_GROUPS):
        yg = y[:, gi * M2_GW:(gi + 1) * M2_GW]
        rs = lax.rsqrt(jnp.mean(yg * yg, axis=-1, keepdims=True) + EPS)
        parts.append(yg * rs)
    mix_m2 = (jnp.concatenate(parts, axis=-1) * m2_norm_w_ref[...]).astype(BF16)

    out = _dot(mix_hg, w_out_ref[0:HG_W, :]) + _dot(mix_m2, w_out_ref[HG_W:HG_W + M2_W, :])
    xr = x + out
    if apply_final_norm:
        ms2 = jnp.mean(xr * xr, axis=-1, keepdims=True)
        xr = (xr * lax.rsqrt(ms2 + EPS)) * final_norm_w_ref[...]
    o_ref[0] = xr.astype(o_ref.dtype)


def _const_spec(shape):
    nd = len(shape)
    return pl.BlockSpec(shape, lambda b, l: (0,) * nd, pipeline_mode=pl.Buffered(1))


def _layer(x, layer, depth, norm_w, w_in, hg_lb_logits, hg_norm_w, conv_w, conv_b, dt_bias, a_log,
           d_skip, m2_norm_w, w_out, final_norm_w):
    bsz, seq, d = x.shape
    assert d == D_MODEL and seq % SEQ_TILE == 0
    assert w_in.shape == (D_MODEL, MAIN_COLS + M2_HEADS)
    tl = SEQ_TILE

    w_main = w_in[:, :MAIN_COLS].astype(BF16)
    wdt = w_in[:, MAIN_COLS:]
    wdt_hi = wdt.astype(BF16)
    wdt_lo = (wdt - wdt_hi.astype(F32)).astype(BF16)
    row = lambda t: t.reshape(1, -1).astype(F32)
    col = lambda t: t.reshape(-1, 1).astype(F32)

    operands = (
        x, row(norm_w), w_main, wdt_hi.T, wdt_lo.T,
        hg_lb_logits.astype(F32), row(hg_norm_w), conv_w.astype(F32), row(conv_b),
        col(dt_bias), col(a_log),
        row(jnp.repeat(d_skip, M2_HEADDIM)), row(m2_norm_w), w_out.astype(BF16), row(final_norm_w),
    )
    in_specs = [pl.BlockSpec((1, tl, d), lambda b, l: (b, l, 0))]
    in_specs += [_const_spec(op.shape) for op in operands[1:]]

    kern = functools.partial(_layer_kernel, layer=layer, apply_final_norm=(layer == depth - 1))
    return pl.pallas_call(
        kern,
        grid=(bsz, seq // tl),
        in_specs=in_specs,
        out_specs=pl.BlockSpec((1, tl, d), lambda b, l: (b, l, 0)),
        out_shape=jax.ShapeDtypeStruct(x.shape, x.dtype),
        scratch_shapes=[
            pltpu.VMEM((HG_HEADS, HG_DV, HG_DK), F32),
            pltpu.VMEM((M2_GROUPS, M2_STATE, M2_GW), F32),
            pltpu.VMEM((CONV_PAD + tl, M2_CONV_DIM), F32),
            pltpu.VMEM((tl, HG_W), F32),
            pltpu.VMEM((tl, M2_W), F32),
        ],
        compiler_params=pltpu.CompilerParams(
            dimension_semantics=("arbitrary", "arbitrary"),
            vmem_limit_bytes=VMEM_LIMIT_BYTES,
        ),
        name="hybrid_layer",
    )(*operands)


@jax.jit
def kernel(x, norm_w, w_in, hg_lb_logits, hg_norm_w, m2_conv_w, m2_conv_b, m2_dt_bias, m2_a_log,
           m2_d_skip, m2_norm_w, w_out, final_norm_w):
    depth = w_in.shape[0]
    for l in range(depth):
        x = _layer(x, l, depth, norm_w[l], w_in[l], hg_lb_logits, hg_norm_w[l], m2_conv_w[l],
                   m2_conv_b[l], m2_dt_bias[l], m2_a_log[l], m2_d_skip[l], m2_norm_w[l], w_out[l],
                   final_norm_w)
    return x
```

```python
import functools

import jax
import jax.numpy as jnp
from jax import lax
from jax.experimental import pallas as pl
from jax.experimental.pallas import tpu as pltpu

F32 = jnp.float32
BF16 = jnp.bfloat16

EPS = 1e-6
D_MODEL = 1024
HG_HEADS = 8
HG_DK = 128
HG_DV = 128
HG_W = HG_HEADS * HG_DK
HG_CHUNK = 64
M2_HEADDIM = 64
M2_W = 1024
M2_HEADS = M2_W // M2_HEADDIM
M2_GROUPS = 2
M2_STATE = 128
M2_CONV = 4
M2_CHUNK = 128
M2_CONV_DIM = M2_W + 2 * M2_GROUPS * M2_STATE
M2_GW = M2_W // M2_GROUPS
M2_HPG = M2_HEADS // M2_GROUPS
M2_QUAD = 4

OFF_Q = 0
OFF_F = OFF_Q + HG_W
OFF_I = OFF_F + HG_W
OFF_G = OFF_I + HG_W
OFF_Z = OFF_G + HG_W
OFF_XBC = OFF_Z + M2_W
OFF_DT = OFF_XBC + M2_CONV_DIM
MAIN_COLS = OFF_DT

SEQ_TILE = 256
CONV_PAD = 8
VMEM_LIMIT_BYTES = 58 * 1024 * 1024
HG_SAFE_LOG_DECAY = -60.0


def _dot(a, b):
    return jnp.dot(a, b, preferred_element_type=F32)


def _dot_nt(a, b):
    return lax.dot_general(a, b, (((1,), (1,)), ((), ())), preferred_element_type=F32)


def _dot_tn(a, b):
    return lax.dot_general(a, b, (((0,), (0,)), ((), ())), preferred_element_type=F32)


def _split3(x):
    hi = x.astype(BF16)
    r1 = x - hi.astype(F32)
    mid = r1.astype(BF16)
    lo = (r1 - mid.astype(F32)).astype(BF16)
    return hi, mid, lo


def _dot_exact_left(mat_bf16, x):
    hi, mid, lo = _split3(x)
    return (_dot(mat_bf16, lo) + _dot(mat_bf16, mid)) + _dot(mat_bf16, hi)


def _dot_exact_right(x, mat_bf16):
    hi, mid, lo = _split3(x)
    return (_dot(lo, mat_bf16) + _dot(mid, mat_bf16)) + _dot(hi, mat_bf16)


def _sigmoid(x):
    return 1.0 / (1.0 + jnp.exp(-x))


def _silu(x):
    return x * _sigmoid(x)


def _softplus(x):
    return jnp.maximum(x, 0.0) + jnp.log(1.0 + jnp.exp(-jnp.abs(x)))


def _iota2(shape):
    return lax.broadcasted_iota(jnp.int32, shape, 0), lax.broadcasted_iota(jnp.int32, shape, 1)


def _block_tril(n, block, upper=False):
    r, c = _iota2((n, n))
    same = (r // block) == (c // block)
    tri = (r <= c) if upper else (c <= r)
    return jnp.where(same & tri, 1.0, 0.0).astype(BF16)


def _rms_rows(o):
    return o * lax.rsqrt(jnp.mean(o * o, axis=-1, keepdims=True) + EPS)


def _hg_same_chunk_exact(q, k, b, tl):
    r, c = _iota2((tl, tl))
    q_bf = q.astype(BF16)
    k_bf = k.astype(BF16)
    atts = [jnp.where(r == c, _dot_nt(q_bf[:, hd * HG_DK:(hd + 1) * HG_DK], k_bf[:, hd * HG_DK:(hd + 1) * HG_DK]), 0.0)
            for hd in range(HG_HEADS)]
    row = lax.broadcasted_iota(jnp.int32, (tl, 1), 0)
    h = HG_CHUNK // 2
    while h >= 1:
        mid = (r // (2 * h)) * (2 * h) + (h - 1)
        sel = jnp.where(c == mid, 1.0, 0.0).astype(BF16)
        ref = _dot_exact_left(sel, b)
        e = jnp.exp(-jnp.abs(b - ref))
        is_q = (row % (2 * h)) >= h
        zq = jnp.where(is_q, q * e, 0.0).astype(BF16)
        zk = jnp.where(is_q, 0.0, k * e).astype(BF16)
        pair = ((r // (2 * h)) == (c // (2 * h))) & ((r % (2 * h)) >= h) & ((c % (2 * h)) < h)
        for hd in range(HG_HEADS):
            cols = slice(hd * HG_DK, (hd + 1) * HG_DK)
            atts[hd] = atts[hd] + jnp.where(pair, _dot_nt(zq[:, cols], zk[:, cols]), 0.0)
        h //= 2
    return atts


def _layer_kernel(x_ref, norm_w_ref, w_main_ref, wdtT_hi_ref, wdtT_lo_ref,
                  lb_logits_ref, hg_norm_w_ref, conv_w_ref, conv_b_ref,
                  dt_bias_col_ref, a_log_col_ref,
                  d_skip_ref, m2_norm_w_ref, w_out_ref, final_norm_w_ref,
                  o_ref,
                  hg_state_ref, m2_state_ref, ubuf_ref, o_hg_ref, y_m2_ref,
                  *, layer, apply_final_norm):
    tl = x_ref.shape[1]
    lt = pl.program_id(1)

    @pl.when(lt == 0)
    def _():
        hg_state_ref[...] = jnp.zeros_like(hg_state_ref)
        m2_state_ref[...] = jnp.zeros_like(m2_state_ref)
        ubuf_ref[0:CONV_PAD, :] = jnp.zeros((CONV_PAD, M2_CONV_DIM), F32)

    x = x_ref[0]
    ms = jnp.mean(x * x, axis=-1, keepdims=True)
    h = (x * lax.rsqrt(ms + EPS)) * norm_w_ref[...]
    h_bf = h.astype(BF16)
    h_lo = (h - h_bf.astype(F32)).astype(BF16)

    def proj(off, width):
        return _dot(h_bf, w_main_ref[:, off:off + width])

    u = proj(OFF_XBC, M2_CONV_DIM)
    fl = proj(OFF_F, HG_W)
    q_raw = proj(OFF_Q, HG_W)
    wdtT_hi = wdtT_hi_ref[...]
    wdtT_lo = wdtT_lo_ref[...]
    dt_rawT = (_dot_nt(wdtT_hi, h_lo) + _dot_nt(wdtT_lo, h_bf)) + _dot_nt(wdtT_hi, h_bf)
    v_bf = proj(OFF_I, HG_W).astype(BF16)
    g_raw = proj(OFF_G, HG_W)
    z_raw = proj(OFF_Z, M2_W)

    logits = lb_logits_ref[...]
    mx = jnp.max(logits, axis=0, keepdims=True)
    ez = jnp.exp(logits - mx)
    lb = jnp.sum(ez[0:layer + 1], axis=0, keepdims=True) / jnp.sum(ez, axis=0, keepdims=True)
    one_m_lb = 1.0 - lb

    sg = _sigmoid(fl)
    f = lb + one_m_lb * sg
    k = one_m_lb * (1.0 - sg)
    b = _dot_exact_left(_block_tril(tl, HG_CHUNK), jnp.log(f))
    q = _silu(q_raw)
    qt = q * jnp.exp(b)
    qt_bf = qt.astype(BF16)
    kt_bf = (k * jnp.exp(-b)).astype(BF16)

    nch = tl // HG_CHUNK
    crow = lambda c: slice(c * HG_CHUNK, (c + 1) * HG_CHUNK)
    tot = [b[(c + 1) * HG_CHUNK - 1:(c + 1) * HG_CHUNK, :] for c in range(nch)]
    start = [jnp.zeros_like(tot[0])]
    for c in range(nch):
        start.append(start[c] + tot[c])
    e_tile = jnp.exp(start[nch])
    zeros_blk = jnp.zeros((HG_CHUNK, HG_W), BF16)
    qg_blocks, kend_blocks = [], []
    qsrc_blocks = [[] for _ in range(nch - 1)]
    ksrc_blocks = [[] for _ in range(nch - 1)]
    for c in range(nch):
        qt_c = qt[crow(c), :]
        khat_c = k[crow(c), :] * jnp.exp(tot[c] - b[crow(c), :])
        qg_blocks.append((qt_c * jnp.exp(start[c])).astype(BF16))
        kend_blocks.append((khat_c * jnp.exp(start[nch] - start[c + 1])).astype(BF16))
        for j in range(nch - 1):
            if c <= j:
                qsrc_blocks[j].append(zeros_blk)
            elif c == j + 1:
                qsrc_blocks[j].append(qt_bf[crow(c), :])
            else:
                qsrc_blocks[j].append((qt_c * jnp.exp(start[c] - start[j + 1])).astype(BF16))
            ksrc_blocks[j].append(khat_c.astype(BF16) if c == j else zeros_blk)
    qg_bf = jnp.concatenate(qg_blocks, axis=0)
    kend_bf = jnp.concatenate(kend_blocks, axis=0)
    qsrc = [jnp.concatenate(blks, axis=0) for blks in qsrc_blocks]
    ksrc = [jnp.concatenate(blks, axis=0) for blks in ksrc_blocks]
    min_tot = tot[0]
    for c in range(1, nch):
        min_tot = jnp.minimum(min_tot, tot[c])
    same_chunk_safe = jnp.min(min_tot) >= HG_SAFE_LOG_DECAY

    ubuf_ref[CONV_PAD:CONV_PAD + tl, :] = u
    conv = u * conv_w_ref[M2_CONV - 1:M2_CONV, :]
    for j in range(M2_CONV - 1):
        sh = M2_CONV - 1 - j
        conv = conv + ubuf_ref[CONV_PAD - sh:CONV_PAD - sh + tl, :] * conv_w_ref[j:j + 1, :]
    ubuf_ref[0:CONV_PAD, :] = ubuf_ref[tl:tl + CONV_PAD, :]
    xbc = _silu(conv + conv_b_ref[...])
    xs = xbc[:, 0:M2_W]
    bm_bf = xbc[:, M2_W:M2_W + M2_GROUPS * M2_STATE].astype(BF16)
    cm_bf = xbc[:, M2_W + M2_GROUPS * M2_STATE:].astype(BF16)

    dtT = _softplus(dt_rawT + dt_bias_col_ref[...])
    aT = dtT * (-jnp.exp(a_log_col_ref[...]))
    csT = _dot_exact_right(aT, _block_tril(tl, M2_CHUNK, upper=True))
    dt = dtT.T
    cs = csT.T

    hr, hc = _iota2((M2_HEADS, M2_W))
    expand = jnp.where(hc // M2_HEADDIM == hr, 1.0, 0.0).astype(BF16)
    r2, c2 = _iota2((M2_CHUNK, M2_CHUNK))
    causal_m2 = c2 <= r2
    qr, qc = _iota2((M2_QUAD * M2_CHUNK, M2_QUAD * M2_HEADDIM))
    quad_mask = (qr // M2_CHUNK) == (qc // M2_HEADDIM)

    for c in range(tl // M2_CHUNK):
        rows = slice(c * M2_CHUNK, (c + 1) * M2_CHUNK)
        cs_c = cs[rows, :]
        cs_last = cs_c[M2_CHUNK - 1:M2_CHUNK, :]
        ecs_x = _dot(jnp.exp(cs_c).astype(BF16), expand)
        dec_x = _dot((jnp.exp(cs_last - cs_c) * dt[rows, :]).astype(BF16), expand)
        xs_c = xs[rows, :]
        xs_bf = xs_c.astype(BF16)
        xdec_bf = (xs_c * dec_x).astype(BF16)
        for gi in range(M2_GROUPS):
            gcols = slice(gi * M2_GW, (gi + 1) * M2_GW)
            ncols = slice(gi * M2_STATE, (gi + 1) * M2_STATE)
            bg = bm_bf[rows, ncols]
            cg = cm_bf[rows, ncols]
            cb = _dot_nt(cg, bg)
            stT = m2_state_ref[gi]
            y_off = _dot(cg, stT.astype(BF16)) * ecs_x[:, gcols]
            for qd in range(M2_HPG // M2_QUAD):
                ws = []
                for hh in range(M2_QUAD):
                    hidx = gi * M2_HPG + qd * M2_QUAD + hh
                    seg = cs_c[:, hidx:hidx + 1] - csT[hidx:hidx + 1, rows]
                    lm = jnp.exp(jnp.where(causal_m2, seg, -jnp.inf))
                    ws.append(((cb * lm) * dtT[hidx:hidx + 1, rows]).astype(BF16))
                q0 = (gi * M2_HPG + qd * M2_QUAD) * M2_HEADDIM
                xq = xs_bf[:, q0:q0 + M2_QUAD * M2_HEADDIM]
                x_bd = jnp.where(quad_mask, jnp.concatenate([xq] * M2_QUAD, axis=0), jnp.zeros_like(quad_mask, BF16))
                y_diag = _dot(jnp.concatenate(ws, axis=1), x_bd)
                o0 = qd * M2_QUAD * M2_HEADDIM
                y_m2_ref[rows, q0:q0 + M2_QUAD * M2_HEADDIM] = y_diag + y_off[:, o0:o0 + M2_QUAD * M2_HEADDIM]
            e_last_x = ecs_x[M2_CHUNK - 1:M2_CHUNK, gcols]
            m2_state_ref[gi] = stT * e_last_x + _dot_tn(bg, xdec_bf[:, gcols])

    y = y_m2_ref[...] + d_skip_ref[...] * xs
    y = y * _silu(z_raw)
    mix_m2 = (jnp.concatenate([_rms_rows(y[:, gi * M2_GW:(gi + 1) * M2_GW]) for gi in range(M2_GROUPS)], axis=-1)
              * m2_norm_w_ref[...]).astype(BF16)

    rr, cc = _iota2((tl, tl))
    diag_causal = (cc <= rr) & ((rr // HG_CHUNK) == (cc // HG_CHUNK))

    def head_cols(hd):
        return slice(hd * HG_DK, (hd + 1) * HG_DK)

    def cross_chunk_att(hd):
        cols = head_cols(hd)
        return _dot_nt(jnp.concatenate([qj[:, cols] for qj in qsrc], axis=1),
                       jnp.concatenate([kj[:, cols] for kj in ksrc], axis=1))

    st_old = []
    for hd in range(HG_HEADS):
        cols = head_cols(hd)
        st = hg_state_ref[hd]
        st_old.append(st.astype(BF16))
        att = jnp.where(diag_causal, _dot_nt(qt_bf[:, cols], kt_bf[:, cols]), 0.0) + cross_chunk_att(hd)
        vh = v_bf[:, cols]
        o = _dot_nt(qg_bf[:, cols], st_old[hd]) + _dot(att.astype(BF16), vh)
        hg_state_ref[hd] = st * e_tile[:, cols] + _dot_tn(vh, kend_bf[:, cols])
        o_hg_ref[:, cols] = _rms_rows(o)

    @pl.when(jnp.logical_not(same_chunk_safe))
    def _():
        same = _hg_same_chunk_exact(q, k, b, tl)
        for hd in range(HG_HEADS):
            cols = head_cols(hd)
            att = same[hd] + cross_chunk_att(hd)
            o = _dot_nt(qg_bf[:, cols], st_old[hd]) + _dot(att.astype(BF16), v_bf[:, cols])
            o_hg_ref[:, cols] = _rms_rows(o)

    mix_hg = (o_hg_ref[...] * hg_norm_w_ref[...] * _silu(g_raw)).astype(BF16)

    out = _dot(mix_hg, w_out_ref[0:HG_W, :]) + _dot(mix_m2, w_out_ref[HG_W:HG_W + M2_W, :])
    xr = x + out
    if apply_final_norm:
        ms2 = jnp.mean(xr * xr, axis=-1, keepdims=True)
        xr = (xr * lax.rsqrt(ms2 + EPS)) * final_norm_w_ref[...]
    o_ref[0] = xr.astype(o_ref.dtype)


def _const_spec(shape):
    nd = len(shape)
    return pl.BlockSpec(shape, lambda b, l: (0,) * nd, pipeline_mode=pl.Buffered(1))


def _layer(x, layer, depth, norm_w, w_in, hg_lb_logits, hg_norm_w, conv_w, conv_b, dt_bias, a_log,
           d_skip, m2_norm_w, w_out, final_norm_w):
    bsz, seq, d = x.shape
    assert d == D_MODEL and seq % SEQ_TILE == 0
    assert w_in.shape == (D_MODEL, MAIN_COLS + M2_HEADS)
    tl = SEQ_TILE

    w_main = w_in[:, :MAIN_COLS].astype(BF16)
    wdt = w_in[:, MAIN_COLS:]
    wdt_hi = wdt.astype(BF16)
    wdt_lo = (wdt - wdt_hi.astype(F32)).astype(BF16)
    row = lambda t: t.reshape(1, -1).astype(F32)
    col = lambda t: t.reshape(-1, 1).astype(F32)

    operands = (
        x, row(norm_w), w_main, wdt_hi.T, wdt_lo.T,
        hg_lb_logits.astype(F32), row(hg_norm_w), conv_w.astype(F32), row(conv_b),
        col(dt_bias), col(a_log),
        row(jnp.repeat(d_skip, M2_HEADDIM)), row(m2_norm_w), w_out.astype(BF16), row(final_norm_w),
    )
    in_specs = [pl.BlockSpec((1, tl, d), lambda b, l: (b, l, 0))]
    in_specs += [_const_spec(op.shape) for op in operands[1:]]

    kern = functools.partial(_layer_kernel, layer=layer, apply_final_norm=(layer == depth - 1))
    return pl.pallas_call(
        kern,
        grid=(bsz, seq // tl),
        in_specs=in_specs,
        out_specs=pl.BlockSpec((1, tl, d), lambda b, l: (b, l, 0)),
        out_shape=jax.ShapeDtypeStruct(x.shape, x.dtype),
        scratch_shapes=[
            pltpu.VMEM((HG_HEADS, HG_DV, HG_DK), F32),
            pltpu.VMEM((M2_GROUPS, M2_STATE, M2_GW), F32),
            pltpu.VMEM((CONV_PAD + tl, M2_CONV_DIM), F32),
            pltpu.VMEM((tl, HG_W), F32),
            pltpu.VMEM((tl, M2_W), F32),
        ],
        compiler_params=pltpu.CompilerParams(
            dimension_semantics=("arbitrary", "arbitrary"),
            vmem_limit_bytes=VMEM_LIMIT_BYTES,
        ),
        name="hybrid_layer",
    )(*operands)


@jax.jit
def kernel(x, norm_w, w_in, hg_lb_logits, hg_norm_w, m2_conv_w, m2_conv_b, m2_dt_bias, m2_a_log,
           m2_d_skip, m2_norm_w, w_out, final_norm_w):
    depth = w_in.shape[0]
    for l in range(depth):
        x = _layer(x, l, depth, norm_w[l], w_in[l], hg_lb_logits, hg_norm_w[l], m2_conv_w[l],
                   m2_conv_b[l], m2_dt_bias[l], m2_a_log[l], m2_d_skip[l], m2_norm_w[l], w_out[l],
                   final_norm_w)
    return x
```

```python
import functools
import math

import jax
import jax.numpy as jnp
from jax import lax
from jax.experimental import pallas as pl
from jax.experimental.pallas import tpu as pltpu

F32 = jnp.float32
BF16 = jnp.bfloat16

EPS = 1e-6
LOG2E = math.log2(math.e)
D_MODEL = 1024
HG_HEADS = 8
HG_DK = 128
HG_DV = 128
HG_W = HG_HEADS * HG_DK
HG_CHUNK = 128
M2_HEADDIM = 64
M2_W = 1024
M2_HEADS = M2_W // M2_HEADDIM
M2_GROUPS = 2
M2_STATE = 128
M2_CONV = 4
M2_CHUNK = 128
M2_CONV_DIM = M2_W + 2 * M2_GROUPS * M2_STATE
M2_GW = M2_W // M2_GROUPS
M2_HPG = M2_HEADS // M2_GROUPS
M2_QUAD = 4

OFF_Q = 0
OFF_F = OFF_Q + HG_W
OFF_I = OFF_F + HG_W
OFF_G = OFF_I + HG_W
OFF_Z = OFF_G + HG_W
OFF_XBC = OFF_Z + M2_W
OFF_DT = OFF_XBC + M2_CONV_DIM
MAIN_COLS = OFF_DT

SEQ_TILE = 2 * HG_CHUNK
CONV_PAD = 8
VMEM_LIMIT_BYTES = 58 * 1024 * 1024
HG_SAFE_LOG2_DECAY = -108.0


def _dot(a, b):
    return jnp.dot(a, b, preferred_element_type=F32)


def _dot_nt(a, b):
    return lax.dot_general(a, b, (((1,), (1,)), ((), ())), preferred_element_type=F32)


def _dot_tn(a, b):
    return lax.dot_general(a, b, (((0,), (0,)), ((), ())), preferred_element_type=F32)


def _split3(x):
    hi = x.astype(BF16)
    r1 = x - hi.astype(F32)
    mid = r1.astype(BF16)
    lo = (r1 - mid.astype(F32)).astype(BF16)
    return hi, mid, lo


def _dot_exact_left(mat_bf16, x):
    hi, mid, lo = _split3(x)
    return (_dot(mat_bf16, lo) + _dot(mat_bf16, mid)) + _dot(mat_bf16, hi)


def _dot_exact_right(x, mat_bf16):
    hi, mid, lo = _split3(x)
    return (_dot(lo, mat_bf16) + _dot(mid, mat_bf16)) + _dot(hi, mat_bf16)


def _sigmoid(x):
    return 1.0 / (1.0 + jnp.exp2(x * (-LOG2E)))


def _silu(x):
    return x * _sigmoid(x)


def _softplus(x):
    return jnp.maximum(x, 0.0) + jnp.log(1.0 + jnp.exp(-jnp.abs(x)))


def _iota2(shape):
    return lax.broadcasted_iota(jnp.int32, shape, 0), lax.broadcasted_iota(jnp.int32, shape, 1)


def _block_tril(n, block, upper=False):
    r, c = _iota2((n, n))
    same = (r // block) == (c // block)
    tri = (r <= c) if upper else (c <= r)
    return jnp.where(same & tri, 1.0, 0.0).astype(BF16)


def _block_diag2(a, b):
    z = jnp.zeros_like(a)
    return jnp.concatenate([jnp.concatenate([a, z], axis=1), jnp.concatenate([z, b], axis=1)], axis=0)


def _chunk_cumsum(x, chunk):
    n, w = x.shape
    sub = lax.broadcasted_iota(jnp.int32, (8, w), 0)
    out = []
    run = None
    for g in range(n // 8):
        blk = x[8 * g:8 * g + 8, :]
        for d in (1, 2, 4):
            blk = blk + jnp.where(sub >= d, pltpu.roll(blk, d, axis=0), 0.0)
        if (8 * g) % chunk != 0:
            blk = blk + run
        run = jnp.broadcast_to(blk[7:8, :], (8, w))
        out.append(blk)
    return jnp.concatenate(out, axis=0)


def _rms_rows(o):
    return o * lax.rsqrt(jnp.mean(o * o, axis=-1, keepdims=True) + EPS)


def _hg_same_chunk_exact(q, k, b2, tl):
    r, c = _iota2((tl, tl))
    q_bf = q.astype(BF16)
    k_bf = k.astype(BF16)
    atts = [jnp.where(r == c, _dot_nt(q_bf[:, hd * HG_DK:(hd + 1) * HG_DK], k_bf[:, hd * HG_DK:(hd + 1) * HG_DK]), 0.0)
            for hd in range(HG_HEADS)]
    row = lax.broadcasted_iota(jnp.int32, (tl, 1), 0)
    h = HG_CHUNK // 2
    while h >= 1:
        mid = (r // (2 * h)) * (2 * h) + (h - 1)
        sel = jnp.where(c == mid, 1.0, 0.0).astype(BF16)
        ref = _dot_exact_left(sel, b2)
        e = jnp.exp2(-jnp.abs(b2 - ref))
        is_q = (row % (2 * h)) >= h
        zq = jnp.where(is_q, q * e, 0.0).astype(BF16)
        zk = jnp.where(is_q, 0.0, k * e).astype(BF16)
        pair = ((r // (2 * h)) == (c // (2 * h))) & ((r % (2 * h)) >= h) & ((c % (2 * h)) < h)
        for hd in range(HG_HEADS):
            cols = slice(hd * HG_DK, (hd + 1) * HG_DK)
            atts[hd] = atts[hd] + jnp.where(pair, _dot_nt(zq[:, cols], zk[:, cols]), 0.0)
        h //= 2
    return atts


def _layer_kernel(x_ref, norm_w_ref, w_main_ref, wdtT_hi_ref, wdtT_lo_ref,
                  lb_logits_ref, hg_norm_w_ref, conv_w_ref, conv_b_ref,
                  dt_bias_col_ref, a_log_col_ref,
                  d_skip_ref, m2_norm_w_ref, w_out_ref, final_norm_w_ref,
                  o_ref,
                  hg_state_ref, m2_state_ref, ubuf_ref, o_hg_ref, y_m2_ref,
                  *, layer, apply_final_norm):
    tl = x_ref.shape[1]
    assert tl == 2 * HG_CHUNK and tl % M2_CHUNK == 0
    lt = pl.program_id(1)

    @pl.when(lt == 0)
    def _():
        hg_state_ref[...] = jnp.zeros_like(hg_state_ref)
        m2_state_ref[...] = jnp.zeros_like(m2_state_ref)
        ubuf_ref[0:CONV_PAD, :] = jnp.zeros((CONV_PAD, M2_CONV_DIM), F32)

    x = x_ref[0]
    ms = jnp.mean(x * x, axis=-1, keepdims=True)
    h = (x * lax.rsqrt(ms + EPS)) * norm_w_ref[...]
    h_bf = h.astype(BF16)
    h_lo = (h - h_bf.astype(F32)).astype(BF16)

    def proj(off, width):
        return _dot(h_bf, w_main_ref[:, off:off + width])

    fl = proj(OFF_F, HG_W)
    q_raw = proj(OFF_Q, HG_W)
    u = proj(OFF_XBC, M2_CONV_DIM)
    wdtT_hi = wdtT_hi_ref[...]
    wdtT_lo = wdtT_lo_ref[...]
    dt_rawT = (_dot_nt(wdtT_hi, h_lo) + _dot_nt(wdtT_lo, h_bf)) + _dot_nt(wdtT_hi, h_bf)
    v_bf = proj(OFF_I, HG_W).astype(BF16)
    g_raw = proj(OFF_G, HG_W)
    z_raw = proj(OFF_Z, M2_W)

    logits = lb_logits_ref[...]
    mx = jnp.max(logits, axis=0, keepdims=True)
    ez = jnp.exp(logits - mx)
    lb = jnp.sum(ez[0:layer + 1], axis=0, keepdims=True) / jnp.sum(ez, axis=0, keepdims=True)
    one_m_lb = 1.0 - lb

    sg = _sigmoid(fl)
    f = lb + one_m_lb * sg
    k = one_m_lb * (1.0 - sg)
    b2 = _chunk_cumsum(jnp.log2(f), HG_CHUNK)
    e = jnp.exp2(b2)
    q = _silu(q_raw)
    qt = q * e
    qt_bf = qt.astype(BF16)
    kt_bf = (k / e).astype(BF16)
    c0 = slice(0, HG_CHUNK)
    c1 = slice(HG_CHUNK, 2 * HG_CHUNK)
    tot0 = b2[HG_CHUNK - 1:HG_CHUNK, :]
    tot1 = b2[2 * HG_CHUNK - 1:2 * HG_CHUNK, :]
    khat0 = k[c0, :] * jnp.exp2(tot0 - b2[c0, :])
    khat1 = k[c1, :] * jnp.exp2(tot1 - b2[c1, :])
    khat0_bf = khat0.astype(BF16)
    kend_bf = jnp.concatenate([(khat0 * jnp.exp2(tot1)).astype(BF16), khat1.astype(BF16)], axis=0)
    qg_bf = jnp.concatenate([qt_bf[c0, :], (qt[c1, :] * jnp.exp2(tot0)).astype(BF16)], axis=0)
    keys1_bf = jnp.concatenate([khat0_bf, kt_bf[c1, :]], axis=0)
    e_tile = jnp.exp2(tot0 + tot1)
    same_chunk_safe = jnp.min(jnp.minimum(tot0, tot1)) >= HG_SAFE_LOG2_DECAY

    ubuf_ref[CONV_PAD:CONV_PAD + tl, :] = u
    conv = u * conv_w_ref[M2_CONV - 1:M2_CONV, :]
    for j in range(M2_CONV - 1):
        sh = M2_CONV - 1 - j
        conv = conv + ubuf_ref[CONV_PAD - sh:CONV_PAD - sh + tl, :] * conv_w_ref[j:j + 1, :]
    ubuf_ref[0:CONV_PAD, :] = ubuf_ref[tl:tl + CONV_PAD, :]
    xbc = _silu(conv + conv_b_ref[...])
    xs = xbc[:, 0:M2_W]
    bm_bf = xbc[:, M2_W:M2_W + M2_GROUPS * M2_STATE].astype(BF16)
    cm_bf = xbc[:, M2_W + M2_GROUPS * M2_STATE:].astype(BF16)

    dtT = _softplus(dt_rawT + dt_bias_col_ref[...])
    aT = dtT * (jnp.exp(a_log_col_ref[...]) * (-LOG2E))
    csT = _dot_exact_right(aT, _block_tril(tl, M2_CHUNK, upper=True))
    dt = dtT.T
    cs = csT.T

    hr, hc = _iota2((M2_HEADS, M2_W))
    expand = jnp.where(hc // M2_HEADDIM == hr, 1.0, 0.0).astype(BF16)
    r2, c2 = _iota2((M2_CHUNK, M2_CHUNK))
    causal_m2 = c2 <= r2
    qr, qc = _iota2((M2_QUAD * M2_CHUNK, M2_QUAD * M2_HEADDIM))
    quad_mask = (qr // M2_CHUNK) == (qc // M2_HEADDIM)

    rc, cc = _iota2((HG_CHUNK, HG_CHUNK))
    causal0 = cc <= rc
    r1, k1 = _iota2((HG_CHUNK, tl))
    causal1 = k1 - HG_CHUNK <= r1

    def head_cols(hd):
        return slice(hd * HG_DK, (hd + 1) * HG_DK)

    o_inter = []
    for p in range(HG_HEADS // 2):
        st_pair = _block_diag2(hg_state_ref[2 * p].astype(BF16), hg_state_ref[2 * p + 1].astype(BF16))
        oi = _dot_nt(qg_bf[:, 2 * p * HG_DK:(2 * p + 2) * HG_DK], st_pair)
        o_inter += [oi[:, 0:HG_DV], oi[:, HG_DV:2 * HG_DV]]
    att0, att1 = [], []
    for hd in range(HG_HEADS):
        cols = head_cols(hd)
        att0.append(jnp.where(causal0, _dot_nt(qt_bf[c0, cols], kt_bf[c0, cols]), 0.0).astype(BF16))
        att1.append(jnp.where(causal1, _dot_nt(qt_bf[c1, cols], keys1_bf[:, cols]), 0.0).astype(BF16))
        hg_state_ref[hd] = hg_state_ref[hd] * e_tile[:, cols] + _dot_tn(v_bf[:, cols], kend_bf[:, cols])

    nck = tl // M2_CHUNK
    rows_of = lambda c: slice(c * M2_CHUNK, (c + 1) * M2_CHUNK)
    ncols_of = lambda gi: slice(gi * M2_STATE, (gi + 1) * M2_STATE)
    gcols_of = lambda gi: slice(gi * M2_GW, (gi + 1) * M2_GW)
    xs_bf = xs.astype(BF16)
    cb = {(c, gi): _dot_nt(cm_bf[rows_of(c), ncols_of(gi)], bm_bf[rows_of(c), ncols_of(gi)])
          for c in range(nck) for gi in range(M2_GROUPS)}
    ecs_x, xdec_bf = [], []
    for c in range(nck):
        cs_c = cs[rows_of(c), :]
        cs_last = cs_c[M2_CHUNK - 1:M2_CHUNK, :]
        ecs_x.append(_dot(jnp.exp2(cs_c).astype(BF16), expand))
        dec_x = _dot((jnp.exp2(cs_last - cs_c) * dt[rows_of(c), :]).astype(BF16), expand)
        xdec_bf.append((xs[rows_of(c), :] * dec_x).astype(BF16))
    y_off = {}
    for gi in range(M2_GROUPS):
        stT = m2_state_ref[gi]
        for c in range(nck):
            cg = cm_bf[rows_of(c), ncols_of(gi)]
            y_off[c, gi] = _dot(cg, stT.astype(BF16)) * ecs_x[c][:, gcols_of(gi)]
            e_last_x = ecs_x[c][M2_CHUNK - 1:M2_CHUNK, gcols_of(gi)]
            stT = stT * e_last_x + _dot_tn(bm_bf[rows_of(c), ncols_of(gi)], xdec_bf[c][:, gcols_of(gi)])
        m2_state_ref[gi] = stT

    for hd in range(HG_HEADS):
        cols = head_cols(hd)
        vh = v_bf[:, cols]
        o_hg_ref[c0, cols] = _rms_rows(o_inter[hd][c0, :] + _dot(att0[hd], vh[c0, :]))
        o_hg_ref[c1, cols] = _rms_rows(o_inter[hd][c1, :] + _dot(att1[hd], vh))

    for c in range(nck):
        rows = rows_of(c)
        for gi in range(M2_GROUPS):
            for qd in range(M2_HPG // M2_QUAD):
                ws = []
                for hh in range(M2_QUAD):
                    hidx = gi * M2_HPG + qd * M2_QUAD + hh
                    seg = cs[rows, hidx:hidx + 1] - csT[hidx:hidx + 1, rows]
                    lm = jnp.exp2(jnp.where(causal_m2, seg, -jnp.inf))
                    ws.append(((cb[c, gi] * lm) * dtT[hidx:hidx + 1, rows]).astype(BF16))
                q0 = (gi * M2_HPG + qd * M2_QUAD) * M2_HEADDIM
                xq = xs_bf[rows, q0:q0 + M2_QUAD * M2_HEADDIM]
                x_bd = jnp.where(quad_mask, jnp.concatenate([xq] * M2_QUAD, axis=0), jnp.zeros_like(quad_mask, BF16))
                y_diag = _dot(jnp.concatenate(ws, axis=1), x_bd)
                o0 = qd * M2_QUAD * M2_HEADDIM
                y_m2_ref[rows, q0:q0 + M2_QUAD * M2_HEADDIM] = y_diag + y_off[c, gi][:, o0:o0 + M2_QUAD * M2_HEADDIM]

    y = y_m2_ref[...] + d_skip_ref[...] * xs
    y = y * _silu(z_raw)
    mix_m2 = (jnp.concatenate([_rms_rows(y[:, gi * M2_GW:(gi + 1) * M2_GW]) for gi in range(M2_GROUPS)], axis=-1)
              * m2_norm_w_ref[...]).astype(BF16)
    out_m2 = _dot(mix_m2, w_out_ref[HG_W:HG_W + M2_W, :])

    @pl.when(jnp.logical_not(same_chunk_safe))
    def _():
        same = _hg_same_chunk_exact(q, k, b2, tl)
        for hd in range(HG_HEADS):
            cols = head_cols(hd)
            vh = v_bf[:, cols]
            cross = _dot_nt(qt_bf[c1, cols], khat0_bf[:, cols])
            o0 = o_inter[hd][c0, :] + _dot(same[hd][c0, c0].astype(BF16), vh[c0, :])
            o1 = (o_inter[hd][c1, :] + _dot(cross.astype(BF16), vh[c0, :])
                  + _dot(same[hd][c1, c1].astype(BF16), vh[c1, :]))
            o_hg_ref[c0, cols] = _rms_rows(o0)
            o_hg_ref[c1, cols] = _rms_rows(o1)

    mix_hg = (o_hg_ref[...] * hg_norm_w_ref[...] * _silu(g_raw)).astype(BF16)

    xr = x + (_dot(mix_hg, w_out_ref[0:HG_W, :]) + out_m2)
    if apply_final_norm:
        ms2 = jnp.mean(xr * xr, axis=-1, keepdims=True)
        xr = (xr * lax.rsqrt(ms2 + EPS)) * final_norm_w_ref[...]
    o_ref[0] = xr.astype(o_ref.dtype)


def _const_spec(shape):
    nd = len(shape)
    return pl.BlockSpec(shape, lambda b, l: (0,) * nd, pipeline_mode=pl.Buffered(1))


def _layer(x, layer, depth, norm_w, w_in, hg_lb_logits, hg_norm_w, conv_w, conv_b, dt_bias, a_log,
           d_skip, m2_norm_w, w_out, final_norm_w):
    bsz, seq, d = x.shape
    assert d == D_MODEL and seq % SEQ_TILE == 0
    assert w_in.shape == (D_MODEL, MAIN_COLS + M2_HEADS)
    tl = SEQ_TILE

    w_main = w_in[:, :MAIN_COLS].astype(BF16)
    wdt = w_in[:, MAIN_COLS:]
    wdt_hi = wdt.astype(BF16)
    wdt_lo = (wdt - wdt_hi.astype(F32)).astype(BF16)
    row = lambda t: t.reshape(1, -1).astype(F32)
    col = lambda t: t.reshape(-1, 1).astype(F32)

    operands = (
        x, row(norm_w), w_main, wdt_hi.T, wdt_lo.T,
        hg_lb_logits.astype(F32), row(hg_norm_w), conv_w.astype(F32), row(conv_b),
        col(dt_bias), col(a_log),
        row(jnp.repeat(d_skip, M2_HEADDIM)), row(m2_norm_w), w_out.astype(BF16), row(final_norm_w),
    )
    in_specs = [pl.BlockSpec((1, tl, d), lambda b, l: (b, l, 0))]
    in_specs += [_const_spec(op.shape) for op in operands[1:]]

    kern = functools.partial(_layer_kernel, layer=layer, apply_final_norm=(layer == depth - 1))
    return pl.pallas_call(
        kern,
        grid=(bsz, seq // tl),
        in_specs=in_specs,
        out_specs=pl.BlockSpec((1, tl, d), lambda b, l: (b, l, 0)),
        out_shape=jax.ShapeDtypeStruct(x.shape, x.dtype),
        scratch_shapes=[
            pltpu.VMEM((HG_HEADS, HG_DV, HG_DK), F32),
            pltpu.VMEM((M2_GROUPS, M2_STATE, M2_GW), F32),
            pltpu.VMEM((CONV_PAD + tl, M2_CONV_DIM), F32),
            pltpu.VMEM((tl, HG_W), F32),
            pltpu.VMEM((tl, M2_W), F32),
        ],
        compiler_params=pltpu.CompilerParams(
            dimension_semantics=("arbitrary", "arbitrary"),
            vmem_limit_bytes=VMEM_LIMIT_BYTES,
        ),
        name="hybrid_layer",
    )(*operands)


@jax.jit
def kernel(x, norm_w, w_in, hg_lb_logits, hg_norm_w, m2_conv_w, m2_conv_b, m2_dt_bias, m2_a_log,
           m2_d_skip, m2_norm_w, w_out, final_norm_w):
    depth = w_in.shape[0]
    for l in range(depth):
        x = _layer(x, l, depth, norm_w[l], w_in[l], hg_lb_logits, hg_norm_w[l], m2_conv_w[l],
                   m2_conv_b[l], m2_dt_bias[l], m2_a_log[l], m2_d_skip[l], m2_norm_w[l], w_out[l],
                   final_norm_w)
    return x
```

```python
import functools
import math

import jax
import jax.numpy as jnp
from jax import lax
from jax.experimental import pallas as pl
from jax.experimental.pallas import tpu as pltpu

F32 = jnp.float32
BF16 = jnp.bfloat16

EPS = 1e-6
LOG2E = math.log2(math.e)
D_MODEL = 1024
HG_HEADS = 8
HG_DK = 128
HG_DV = 128
HG_W = HG_HEADS * HG_DK
HG_CHUNK = 128
M2_HEADDIM = 64
M2_W = 1024
M2_HEADS = M2_W // M2_HEADDIM
M2_GROUPS = 2
M2_STATE = 128
M2_CONV = 4
M2_CHUNK = 128
M2_CONV_DIM = M2_W + 2 * M2_GROUPS * M2_STATE
M2_GW = M2_W // M2_GROUPS
M2_HPG = M2_HEADS // M2_GROUPS
M2_QUAD = 4

OFF_Q = 0
OFF_F = OFF_Q + HG_W
OFF_I = OFF_F + HG_W
OFF_G = OFF_I + HG_W
OFF_Z = OFF_G + HG_W
OFF_XBC = OFF_Z + M2_W
OFF_DT = OFF_XBC + M2_CONV_DIM
MAIN_COLS = OFF_DT

SEQ_TILE = 2 * HG_CHUNK
STREAMS = 1
CONV_PAD = 8
VMEM_LIMIT_BYTES = 58 * 1024 * 1024
HG_SAFE_LOG2_DECAY = -108.0


def _dot(a, b):
    return jnp.dot(a, b, preferred_element_type=F32)


def _dot_nt(a, b):
    return lax.dot_general(a, b, (((1,), (1,)), ((), ())), preferred_element_type=F32)


def _dot_tn(a, b):
    return lax.dot_general(a, b, (((0,), (0,)), ((), ())), preferred_element_type=F32)


def _split3(x):
    hi = x.astype(BF16)
    r1 = x - hi.astype(F32)
    mid = r1.astype(BF16)
    lo = (r1 - mid.astype(F32)).astype(BF16)
    return hi, mid, lo


def _dot_exact_left(mat_bf16, x):
    hi, mid, lo = _split3(x)
    return (_dot(mat_bf16, lo) + _dot(mat_bf16, mid)) + _dot(mat_bf16, hi)


def _dot_exact_right(x, mat_bf16):
    hi, mid, lo = _split3(x)
    return (_dot(lo, mat_bf16) + _dot(mid, mat_bf16)) + _dot(hi, mat_bf16)


def _sigmoid(x):
    return 1.0 / (1.0 + jnp.exp2(x * (-LOG2E)))


def _silu(x):
    return x * _sigmoid(x)


def _softplus(x):
    return jnp.maximum(x, 0.0) + jnp.log(1.0 + jnp.exp(-jnp.abs(x)))


def _iota2(shape):
    return lax.broadcasted_iota(jnp.int32, shape, 0), lax.broadcasted_iota(jnp.int32, shape, 1)


def _block_tril(n, block, upper=False):
    r, c = _iota2((n, n))
    same = (r // block) == (c // block)
    tri = (r <= c) if upper else (c <= r)
    return jnp.where(same & tri, 1.0, 0.0).astype(BF16)


def _block_diag2(a, b):
    z = jnp.zeros_like(a)
    return jnp.concatenate([jnp.concatenate([a, z], axis=1), jnp.concatenate([z, b], axis=1)], axis=0)


def _chunk_cumsum(x, chunk):
    n, w = x.shape
    sub = lax.broadcasted_iota(jnp.int32, (8, w), 0)
    out = []
    run = None
    for g in range(n // 8):
        blk = x[8 * g:8 * g + 8, :]
        for d in (1, 2, 4):
            blk = blk + jnp.where(sub >= d, pltpu.roll(blk, d, axis=0), 0.0)
        if (8 * g) % chunk != 0:
            blk = blk + run
        run = jnp.broadcast_to(blk[7:8, :], (8, w))
        out.append(blk)
    return jnp.concatenate(out, axis=0)


def _rms_rows(o):
    return o * lax.rsqrt(jnp.mean(o * o, axis=-1, keepdims=True) + EPS)


def _hg_same_chunk_exact(q, k, b2, tl):
    r, c = _iota2((tl, tl))
    q_bf = q.astype(BF16)
    k_bf = k.astype(BF16)
    atts = [jnp.where(r == c, _dot_nt(q_bf[:, hd * HG_DK:(hd + 1) * HG_DK], k_bf[:, hd * HG_DK:(hd + 1) * HG_DK]), 0.0)
            for hd in range(HG_HEADS)]
    row = lax.broadcasted_iota(jnp.int32, (tl, 1), 0)
    h = HG_CHUNK // 2
    while h >= 1:
        mid = (r // (2 * h)) * (2 * h) + (h - 1)
        sel = jnp.where(c == mid, 1.0, 0.0).astype(BF16)
        ref = _dot_exact_left(sel, b2)
        e = jnp.exp2(-jnp.abs(b2 - ref))
        is_q = (row % (2 * h)) >= h
        zq = jnp.where(is_q, q * e, 0.0).astype(BF16)
        zk = jnp.where(is_q, 0.0, k * e).astype(BF16)
        pair = ((r // (2 * h)) == (c // (2 * h))) & ((r % (2 * h)) >= h) & ((c % (2 * h)) < h)
        for hd in range(HG_HEADS):
            cols = slice(hd * HG_DK, (hd + 1) * HG_DK)
            atts[hd] = atts[hd] + jnp.where(pair, _dot_nt(zq[:, cols], zk[:, cols]), 0.0)
        h //= 2
    return atts


class _Vals(dict):
    __getattr__ = dict.__getitem__
    __setattr__ = dict.__setitem__


def _front(x, p, st, layer):
    tl = x.shape[0]
    v = _Vals(x=x)
    ms = jnp.mean(x * x, axis=-1, keepdims=True)
    h = (x * lax.rsqrt(ms + EPS)) * p.norm_w[...]
    h_bf = h.astype(BF16)
    h_lo = (h - h_bf.astype(F32)).astype(BF16)

    def proj(off, width):
        return _dot(h_bf, p.w_main[:, off:off + width])

    fl = proj(OFF_F, HG_W)
    wdtT_hi = p.wdtT_hi[...]
    wdtT_lo = p.wdtT_lo[...]
    dt_rawT = (_dot_nt(wdtT_hi, h_lo) + _dot_nt(wdtT_lo, h_bf)) + _dot_nt(wdtT_hi, h_bf)
    q_raw = proj(OFF_Q, HG_W)
    v.dtT = _softplus(dt_rawT + p.dt_bias_col[...])
    aT = v.dtT * (jnp.exp(p.a_log_col[...]) * (-LOG2E))
    u = proj(OFF_XBC, M2_CONV_DIM)
    v.csT = _dot_exact_right(aT, _block_tril(tl, M2_CHUNK, upper=True))
    v.vT_bf = _dot_nt(p.w_iT[...], h_bf).astype(BF16)
    v.gate_hg = _silu(proj(OFF_G, HG_W)).astype(BF16)
    v.gate_m2 = _silu(proj(OFF_Z, M2_W)).astype(BF16)

    logits = p.lb_logits[...]
    mx = jnp.max(logits, axis=0, keepdims=True)
    ez = jnp.exp(logits - mx)
    lb = jnp.sum(ez[0:layer + 1], axis=0, keepdims=True) / jnp.sum(ez, axis=0, keepdims=True)
    one_m_lb = 1.0 - lb
    sg = _sigmoid(fl)
    f = lb + one_m_lb * sg
    k = one_m_lb * (1.0 - sg)
    b2 = _chunk_cumsum(jnp.log2(f), HG_CHUNK)
    e = jnp.exp2(b2)
    q = _silu(q_raw)
    qt = q * e
    v.q, v.k, v.b2 = q, k, b2
    v.qt_bf = qt.astype(BF16)
    v.kt_bf = (k / e).astype(BF16)
    c0, c1 = _C0, _C1
    tot0 = b2[HG_CHUNK - 1:HG_CHUNK, :]
    tot1 = b2[2 * HG_CHUNK - 1:2 * HG_CHUNK, :]
    khat0 = k[c0, :] * jnp.exp2(tot0 - b2[c0, :])
    khat1 = k[c1, :] * jnp.exp2(tot1 - b2[c1, :])
    v.khat0_bf = khat0.astype(BF16)
    v.kend_bf = jnp.concatenate([(khat0 * jnp.exp2(tot1)).astype(BF16), khat1.astype(BF16)], axis=0)
    v.qg_bf = jnp.concatenate([v.qt_bf[c0, :], (qt[c1, :] * jnp.exp2(tot0)).astype(BF16)], axis=0)
    v.keys1_bf = jnp.concatenate([v.khat0_bf, v.kt_bf[c1, :]], axis=0)
    v.e_tile = jnp.exp2(tot0 + tot1)
    v.same_chunk_safe = jnp.min(jnp.minimum(tot0, tot1)) >= HG_SAFE_LOG2_DECAY

    st.ubuf[CONV_PAD:CONV_PAD + tl, :] = u
    conv = u * p.conv_w[M2_CONV - 1:M2_CONV, :]
    for j in range(M2_CONV - 1):
        sh = M2_CONV - 1 - j
        conv = conv + st.ubuf[CONV_PAD - sh:CONV_PAD - sh + tl, :] * p.conv_w[j:j + 1, :]
    st.ubuf[0:CONV_PAD, :] = st.ubuf[tl:tl + CONV_PAD, :]
    xbc = _silu(conv + p.conv_b[...])
    v.xs = xbc[:, 0:M2_W]
    v.bm_bf = xbc[:, M2_W:M2_W + M2_GROUPS * M2_STATE].astype(BF16)
    v.cm_bf = xbc[:, M2_W + M2_GROUPS * M2_STATE:].astype(BF16)
    v.dt = v.dtT.T
    v.cs = v.csT.T
    return v


def _head_cols(hd):
    return slice(hd * HG_DK, (hd + 1) * HG_DK)


def _store_mix_hg(v, p, st, hd, o0, o1):
    cols = _head_cols(hd)
    for rows, o in ((_C0, o0), (_C1, o1)):
        st.mix_hg[rows, cols] = (_rms_rows(o) * p.hg_norm_w[:, cols] * v.gate_hg[rows, cols]).astype(BF16)


_C0 = slice(0, HG_CHUNK)
_C1 = slice(HG_CHUNK, 2 * HG_CHUNK)


def _waves(v, p, st):
    tl = v.x.shape[0]
    c0, c1 = _C0, _C1
    rc, cc = _iota2((HG_CHUNK, HG_CHUNK))
    causal0 = cc <= rc
    r1, k1 = _iota2((HG_CHUNK, tl))
    causal1 = k1 - HG_CHUNK <= r1
    v.o_inter = []
    for pr in range(HG_HEADS // 2):
        st_pair = _block_diag2(st.hg_state[2 * pr].astype(BF16), st.hg_state[2 * pr + 1].astype(BF16))
        oi = _dot_nt(v.qg_bf[:, 2 * pr * HG_DK:(2 * pr + 2) * HG_DK], st_pair)
        v.o_inter += [oi[:, 0:HG_DV], oi[:, HG_DV:2 * HG_DV]]
    att0, att1 = [], []
    for hd in range(HG_HEADS):
        cols = _head_cols(hd)
        att0.append(jnp.where(causal0, _dot_nt(v.qt_bf[c0, cols], v.kt_bf[c0, cols]), 0.0).astype(BF16))
        att1.append(jnp.where(causal1, _dot_nt(v.qt_bf[c1, cols], v.keys1_bf[:, cols]), 0.0).astype(BF16))
    for hd in range(HG_HEADS):
        cols = _head_cols(hd)
        st.hg_state[hd] = st.hg_state[hd] * v.e_tile[:, cols] + _dot(v.vT_bf[cols, :], v.kend_bf[:, cols])

    hr, hc = _iota2((M2_HEADS, M2_W))
    expand = jnp.where(hc // M2_HEADDIM == hr, 1.0, 0.0).astype(BF16)
    r2, c2 = _iota2((M2_CHUNK, M2_CHUNK))
    causal_m2 = c2 <= r2
    qr, qc = _iota2((M2_QUAD * M2_CHUNK, M2_QUAD * M2_HEADDIM))
    quad_mask = (qr // M2_CHUNK) == (qc // M2_HEADDIM)
    nck = tl // M2_CHUNK
    rows_of = lambda c: slice(c * M2_CHUNK, (c + 1) * M2_CHUNK)
    ncols_of = lambda gi: slice(gi * M2_STATE, (gi + 1) * M2_STATE)
    gcols_of = lambda gi: slice(gi * M2_GW, (gi + 1) * M2_GW)
    xs, cs, csT, dtT = v.xs, v.cs, v.csT, v.dtT
    xs_bf = xs.astype(BF16)
    cb = {(c, gi): _dot_nt(v.cm_bf[rows_of(c), ncols_of(gi)], v.bm_bf[rows_of(c), ncols_of(gi)])
          for c in range(nck) for gi in range(M2_GROUPS)}
    ecs_x, xdec_bf = [], []
    for c in range(nck):
        cs_c = cs[rows_of(c), :]
        cs_last = cs_c[M2_CHUNK - 1:M2_CHUNK, :]
        ecs_x.append(_dot(jnp.exp2(cs_c).astype(BF16), expand))
        dec_x = _dot((jnp.exp2(cs_last - cs_c) * v.dt[rows_of(c), :]).astype(BF16), expand)
        xdec_bf.append((xs[rows_of(c), :] * dec_x).astype(BF16))
    y_off = {}
    for gi in range(M2_GROUPS):
        stT = st.m2_state[gi]
        for c in range(nck):
            cg = v.cm_bf[rows_of(c), ncols_of(gi)]
            y_off[c, gi] = _dot(cg, stT.astype(BF16)) * ecs_x[c][:, gcols_of(gi)]
            e_last_x = ecs_x[c][M2_CHUNK - 1:M2_CHUNK, gcols_of(gi)]
            stT = stT * e_last_x + _dot_tn(v.bm_bf[rows_of(c), ncols_of(gi)], xdec_bf[c][:, gcols_of(gi)])
        st.m2_state[gi] = stT

    for hd in range(HG_HEADS):
        cols = _head_cols(hd)
        vT = v.vT_bf[cols, :]
        _store_mix_hg(v, p, st, hd, v.o_inter[hd][c0, :] + _dot_nt(att0[hd], vT[:, c0]),
                      v.o_inter[hd][c1, :] + _dot_nt(att1[hd], vT))

    y_rows = []
    for c in range(nck):
        rows = rows_of(c)
        y_quads = []
        for gi in range(M2_GROUPS):
            for qd in range(M2_HPG // M2_QUAD):
                ws = []
                for hh in range(M2_QUAD):
                    hidx = gi * M2_HPG + qd * M2_QUAD + hh
                    seg = cs[rows, hidx:hidx + 1] - csT[hidx:hidx + 1, rows]
                    lm = jnp.exp2(jnp.where(causal_m2, seg, -jnp.inf))
                    ws.append(((cb[c, gi] * lm) * dtT[hidx:hidx + 1, rows]).astype(BF16))
                q0 = (gi * M2_HPG + qd * M2_QUAD) * M2_HEADDIM
                xq = xs_bf[rows, q0:q0 + M2_QUAD * M2_HEADDIM]
                x_bd = jnp.where(quad_mask, jnp.concatenate([xq] * M2_QUAD, axis=0), jnp.zeros_like(quad_mask, BF16))
                y_diag = _dot(jnp.concatenate(ws, axis=1), x_bd)
                o0 = qd * M2_QUAD * M2_HEADDIM
                y_quads.append(y_diag + y_off[c, gi][:, o0:o0 + M2_QUAD * M2_HEADDIM])
        y_rows.append(jnp.concatenate(y_quads, axis=1))

    y = jnp.concatenate(y_rows, axis=0) + p.d_skip[...] * xs
    y = y * v.gate_m2
    mix_m2 = (jnp.concatenate([_rms_rows(y[:, gi * M2_GW:(gi + 1) * M2_GW]) for gi in range(M2_GROUPS)], axis=-1)
              * p.m2_norm_w[...]).astype(BF16)
    v.out_m2 = _dot(mix_m2, p.w_out[HG_W:HG_W + M2_W, :])


def _redo_heads_exact(v, p, st):
    c0, c1 = _C0, _C1
    same = _hg_same_chunk_exact(v.q, v.k, v.b2, v.x.shape[0])
    for hd in range(HG_HEADS):
        cols = _head_cols(hd)
        vT = v.vT_bf[cols, :]
        cross = _dot_nt(v.qt_bf[c1, cols], v.khat0_bf[:, cols])
        o0 = v.o_inter[hd][c0, :] + _dot_nt(same[hd][c0, c0].astype(BF16), vT[:, c0])
        o1 = (v.o_inter[hd][c1, :] + _dot_nt(cross.astype(BF16), vT[:, c0])
              + _dot_nt(same[hd][c1, c1].astype(BF16), vT[:, c1]))
        _store_mix_hg(v, p, st, hd, o0, o1)


def _tail(v, p, st, apply_final_norm):
    xr = v.x + (_dot(st.mix_hg[...], p.w_out[0:HG_W, :]) + v.out_m2)
    if apply_final_norm:
        ms2 = jnp.mean(xr * xr, axis=-1, keepdims=True)
        xr = (xr * lax.rsqrt(ms2 + EPS)) * p.final_norm_w[...]
    return xr


def _layer_kernel(x_ref, norm_w_ref, w_main_ref, w_iT_ref, wdtT_hi_ref, wdtT_lo_ref,
                  lb_logits_ref, hg_norm_w_ref, conv_w_ref, conv_b_ref,
                  dt_bias_col_ref, a_log_col_ref,
                  d_skip_ref, m2_norm_w_ref, w_out_ref, final_norm_w_ref,
                  o_ref,
                  hg_state_ref, m2_state_ref, ubuf_ref, mix_hg_ref,
                  *, layer, apply_final_norm):
    nstream, tl = x_ref.shape[0], x_ref.shape[1]
    assert tl == 2 * HG_CHUNK and tl % M2_CHUNK == 0
    p = _Vals(norm_w=norm_w_ref, w_main=w_main_ref, w_iT=w_iT_ref, wdtT_hi=wdtT_hi_ref, wdtT_lo=wdtT_lo_ref,
              lb_logits=lb_logits_ref, hg_norm_w=hg_norm_w_ref, conv_w=conv_w_ref, conv_b=conv_b_ref,
              dt_bias_col=dt_bias_col_ref, a_log_col=a_log_col_ref, d_skip=d_skip_ref,
              m2_norm_w=m2_norm_w_ref, w_out=w_out_ref, final_norm_w=final_norm_w_ref)
    sts = [_Vals(hg_state=hg_state_ref.at[s], m2_state=m2_state_ref.at[s], ubuf=ubuf_ref.at[s],
                 mix_hg=mix_hg_ref.at[s]) for s in range(nstream)]

    @pl.when(pl.program_id(1) == 0)
    def _():
        hg_state_ref[...] = jnp.zeros_like(hg_state_ref)
        m2_state_ref[...] = jnp.zeros_like(m2_state_ref)
        for s in range(nstream):
            ubuf_ref[s, 0:CONV_PAD, :] = jnp.zeros((CONV_PAD, M2_CONV_DIM), F32)

    vals = []
    for s in range(nstream):
        vals.append(_front(x_ref[s], p, sts[s], layer))
        _waves(vals[s], p, sts[s])

    for s in range(nstream):
        @pl.when(jnp.logical_not(vals[s].same_chunk_safe))
        def _(s=s):
            _redo_heads_exact(vals[s], p, sts[s])

    for s in range(nstream):
        o_ref[s] = _tail(vals[s], p, sts[s], apply_final_norm).astype(o_ref.dtype)


def _const_spec(shape):
    nd = len(shape)
    return pl.BlockSpec(shape, lambda b, l: (0,) * nd, pipeline_mode=pl.Buffered(1))


def _layer(x, layer, depth, norm_w, w_in, hg_lb_logits, hg_norm_w, conv_w, conv_b, dt_bias, a_log,
           d_skip, m2_norm_w, w_out, final_norm_w):
    bsz, seq, d = x.shape
    assert d == D_MODEL and seq % SEQ_TILE == 0 and bsz % STREAMS == 0
    assert w_in.shape == (D_MODEL, MAIN_COLS + M2_HEADS)
    tl, ns = SEQ_TILE, STREAMS

    w_main = w_in.astype(BF16)
    wdt = w_in[:, MAIN_COLS:]
    wdt_hi = wdt.astype(BF16)
    wdt_lo = (wdt - wdt_hi.astype(F32)).astype(BF16)
    row = lambda t: t.reshape(1, -1).astype(F32)
    col = lambda t: t.reshape(-1, 1).astype(F32)

    operands = (
        x, row(norm_w), w_main, w_in[:, OFF_I:OFF_I + HG_W].T.astype(BF16), wdt_hi.T, wdt_lo.T,
        hg_lb_logits.astype(F32), row(hg_norm_w), conv_w.astype(F32), row(conv_b),
        col(dt_bias), col(a_log),
        row(jnp.repeat(d_skip, M2_HEADDIM)), row(m2_norm_w), w_out.astype(BF16), row(final_norm_w),
    )
    in_specs = [pl.BlockSpec((ns, tl, d), lambda b, l: (b, l, 0))]
    in_specs += [_const_spec(op.shape) for op in operands[1:]]

    kern = functools.partial(_layer_kernel, layer=layer, apply_final_norm=(layer == depth - 1))
    return pl.pallas_call(
        kern,
        grid=(bsz // ns, seq // tl),
        in_specs=in_specs,
        out_specs=pl.BlockSpec((ns, tl, d), lambda b, l: (b, l, 0)),
        out_shape=jax.ShapeDtypeStruct(x.shape, x.dtype),
        scratch_shapes=[
            pltpu.VMEM((ns, HG_HEADS, HG_DV, HG_DK), F32),
            pltpu.VMEM((ns, M2_GROUPS, M2_STATE, M2_GW), F32),
            pltpu.VMEM((ns, CONV_PAD + tl, M2_CONV_DIM), F32),
            pltpu.VMEM((ns, tl, HG_W), BF16),
        ],
        compiler_params=pltpu.CompilerParams(
            dimension_semantics=("arbitrary", "arbitrary"),
            vmem_limit_bytes=VMEM_LIMIT_BYTES,
        ),
        name="hybrid_layer",
    )(*operands)


@jax.jit
def kernel(x, norm_w, w_in, hg_lb_logits, hg_norm_w, m2_conv_w, m2_conv_b, m2_dt_bias, m2_a_log,
           m2_d_skip, m2_norm_w, w_out, final_norm_w):
    depth = w_in.shape[0]
    for l in range(depth):
        x = _layer(x, l, depth, norm_w[l], w_in[l], hg_lb_logits, hg_norm_w[l], m2_conv_w[l],
                   m2_conv_b[l], m2_dt_bias[l], m2_a_log[l], m2_d_skip[l], m2_norm_w[l], w_out[l],
                   final_norm_w)
    return x
```

```python
import functools
import math

import jax
import jax.numpy as jnp
from jax import lax
from jax.experimental import pallas as pl
from jax.experimental.pallas import tpu as pltpu

F32 = jnp.float32
BF16 = jnp.bfloat16

EPS = 1e-6
LOG2E = math.log2(math.e)
D_MODEL = 1024
HG_HEADS = 8
HG_DK = 128
HG_DV = 128
HG_W = HG_HEADS * HG_DK
HG_CHUNK = 128
M2_HEADDIM = 64
M2_W = 1024
M2_HEADS = M2_W // M2_HEADDIM
M2_GROUPS = 2
M2_STATE = 128
M2_CONV = 4
M2_CHUNK = 128
M2_CONV_DIM = M2_W + 2 * M2_GROUPS * M2_STATE
M2_GW = M2_W // M2_GROUPS
M2_HPG = M2_HEADS // M2_GROUPS
M2_QUAD = 4

OFF_Q = 0
OFF_F = OFF_Q + HG_W
OFF_I = OFF_F + HG_W
OFF_G = OFF_I + HG_W
OFF_Z = OFF_G + HG_W
OFF_XBC = OFF_Z + M2_W
OFF_DT = OFF_XBC + M2_CONV_DIM
MAIN_COLS = OFF_DT

SEQ_TILE = 2 * HG_CHUNK
STREAMS = 1
CONV_PAD = 8
W_IN_STAGE_ROWS = 64
W_OUT_STAGE_ROWS = 256
VMEM_LIMIT_BYTES = 58 * 1024 * 1024
HG_SAFE_LOG2_DECAY = -108.0


def _dot(a, b):
    return jnp.dot(a, b, preferred_element_type=F32)


def _dot_nt(a, b):
    return lax.dot_general(a, b, (((1,), (1,)), ((), ())), preferred_element_type=F32)


def _dot_tn(a, b):
    return lax.dot_general(a, b, (((0,), (0,)), ((), ())), preferred_element_type=F32)


def _split3(x):
    hi = x.astype(BF16)
    r1 = x - hi.astype(F32)
    mid = r1.astype(BF16)
    lo = (r1 - mid.astype(F32)).astype(BF16)
    return hi, mid, lo


def _dot_exact_left(mat_bf16, x):
    hi, mid, lo = _split3(x)
    return (_dot(mat_bf16, lo) + _dot(mat_bf16, mid)) + _dot(mat_bf16, hi)


def _dot_exact_right(x, mat_bf16):
    hi, mid, lo = _split3(x)
    return (_dot(lo, mat_bf16) + _dot(mid, mat_bf16)) + _dot(hi, mat_bf16)


def _sigmoid(x):
    return 1.0 / (1.0 + jnp.exp2(x * (-LOG2E)))


def _silu(x):
    return x * _sigmoid(x)


def _softplus(x):
    return jnp.maximum(x, 0.0) + jnp.log(1.0 + jnp.exp(-jnp.abs(x)))


def _iota2(shape):
    return lax.broadcasted_iota(jnp.int32, shape, 0), lax.broadcasted_iota(jnp.int32, shape, 1)


def _block_tril(n, block, upper=False):
    r, c = _iota2((n, n))
    same = (r // block) == (c // block)
    tri = (r <= c) if upper else (c <= r)
    return jnp.where(same & tri, 1.0, 0.0).astype(BF16)


def _block_diag2(a, b):
    z = jnp.zeros_like(a)
    return jnp.concatenate([jnp.concatenate([a, z], axis=1), jnp.concatenate([z, b], axis=1)], axis=0)


def _chunk_cumsum(x, chunk):
    n, w = x.shape
    sub = lax.broadcasted_iota(jnp.int32, (8, w), 0)
    out = []
    run = None
    for g in range(n // 8):
        blk = x[8 * g:8 * g + 8, :]
        for d in (1, 2, 4):
            blk = blk + jnp.where(sub >= d, pltpu.roll(blk, d, axis=0), 0.0)
        if (8 * g) % chunk != 0:
            blk = blk + run
        run = jnp.broadcast_to(blk[7:8, :], (8, w))
        out.append(blk)
    return jnp.concatenate(out, axis=0)


def _rms_rows(o):
    return o * lax.rsqrt(jnp.mean(o * o, axis=-1, keepdims=True) + EPS)


def _hg_same_chunk_exact(q, k, b2, tl):
    r, c = _iota2((tl, tl))
    q_bf = q.astype(BF16)
    k_bf = k.astype(BF16)
    atts = [jnp.where(r == c, _dot_nt(q_bf[:, hd * HG_DK:(hd + 1) * HG_DK], k_bf[:, hd * HG_DK:(hd + 1) * HG_DK]), 0.0)
            for hd in range(HG_HEADS)]
    row = lax.broadcasted_iota(jnp.int32, (tl, 1), 0)
    h = HG_CHUNK // 2
    while h >= 1:
        mid = (r // (2 * h)) * (2 * h) + (h - 1)
        sel = jnp.where(c == mid, 1.0, 0.0).astype(BF16)
        ref = _dot_exact_left(sel, b2)
        e = jnp.exp2(-jnp.abs(b2 - ref))
        is_q = (row % (2 * h)) >= h
        zq = jnp.where(is_q, q * e, 0.0).astype(BF16)
        zk = jnp.where(is_q, 0.0, k * e).astype(BF16)
        pair = ((r // (2 * h)) == (c // (2 * h))) & ((r % (2 * h)) >= h) & ((c % (2 * h)) < h)
        for hd in range(HG_HEADS):
            cols = slice(hd * HG_DK, (hd + 1) * HG_DK)
            atts[hd] = atts[hd] + jnp.where(pair, _dot_nt(zq[:, cols], zk[:, cols]), 0.0)
        h //= 2
    return atts


class _Vals(dict):
    __getattr__ = dict.__getitem__
    __setattr__ = dict.__setitem__


def _front(x, p, st, layer):
    tl = x.shape[0]
    v = _Vals(x=x)
    ms = jnp.mean(x * x, axis=-1, keepdims=True)
    h = (x * lax.rsqrt(ms + EPS)) * p.norm_w[...]
    h_bf = h.astype(BF16)
    h_lo = (h - h_bf.astype(F32)).astype(BF16)

    def proj(off, width):
        return _dot(h_bf, p.w_main[:, off:off + width])

    fl = proj(OFF_F, HG_W)
    wdtT_hi = p.wdtT_hi[...]
    wdtT_lo = p.wdtT_lo[...]
    dt_rawT = (_dot_nt(wdtT_hi, h_lo) + _dot_nt(wdtT_lo, h_bf)) + _dot_nt(wdtT_hi, h_bf)
    q_raw = proj(OFF_Q, HG_W)
    v.dtT = _softplus(dt_rawT + p.dt_bias_col[...])
    aT = v.dtT * (jnp.exp(p.a_log_col[...]) * (-LOG2E))
    u = proj(OFF_XBC, M2_CONV_DIM)
    v.csT = _dot_exact_right(aT, _block_tril(tl, M2_CHUNK, upper=True))
    v.vT_bf = _dot_nt(p.w_iT[...], h_bf).astype(BF16)
    v.gate_hg = _silu(proj(OFF_G, HG_W)).astype(BF16)
    v.gate_m2 = _silu(proj(OFF_Z, M2_W)).astype(BF16)

    logits = p.lb_logits[...]
    mx = jnp.max(logits, axis=0, keepdims=True)
    ez = jnp.exp(logits - mx)
    lb = jnp.sum(ez[0:layer + 1], axis=0, keepdims=True) / jnp.sum(ez, axis=0, keepdims=True)
    one_m_lb = 1.0 - lb
    sg = _sigmoid(fl)
    f = lb + one_m_lb * sg
    k = one_m_lb * (1.0 - sg)
    b2 = _chunk_cumsum(jnp.log2(f), HG_CHUNK)
    e = jnp.exp2(b2)
    q = _silu(q_raw)
    qt = q * e
    v.q, v.k, v.b2 = q, k, b2
    v.qt_bf = qt.astype(BF16)
    v.kt_bf = (k / e).astype(BF16)
    c0, c1 = _C0, _C1
    tot0 = b2[HG_CHUNK - 1:HG_CHUNK, :]
    tot1 = b2[2 * HG_CHUNK - 1:2 * HG_CHUNK, :]
    khat0 = k[c0, :] * jnp.exp2(tot0 - b2[c0, :])
    khat1 = k[c1, :] * jnp.exp2(tot1 - b2[c1, :])
    v.khat0_bf = khat0.astype(BF16)
    v.kend_bf = jnp.concatenate([(khat0 * jnp.exp2(tot1)).astype(BF16), khat1.astype(BF16)], axis=0)
    v.qg_bf = jnp.concatenate([v.qt_bf[c0, :], (qt[c1, :] * jnp.exp2(tot0)).astype(BF16)], axis=0)
    v.keys1_bf = jnp.concatenate([v.khat0_bf, v.kt_bf[c1, :]], axis=0)
    v.e_tile = jnp.exp2(tot0 + tot1)
    v.same_chunk_safe = jnp.min(jnp.minimum(tot0, tot1)) >= HG_SAFE_LOG2_DECAY

    st.ubuf[CONV_PAD:CONV_PAD + tl, :] = u
    conv = u * p.conv_w[M2_CONV - 1:M2_CONV, :]
    for j in range(M2_CONV - 1):
        sh = M2_CONV - 1 - j
        conv = conv + st.ubuf[CONV_PAD - sh:CONV_PAD - sh + tl, :] * p.conv_w[j:j + 1, :]
    st.ubuf[0:CONV_PAD, :] = st.ubuf[tl:tl + CONV_PAD, :]
    xbc = _silu(conv + p.conv_b[...])
    v.xs = xbc[:, 0:M2_W]
    v.bm_bf = xbc[:, M2_W:M2_W + M2_GROUPS * M2_STATE].astype(BF16)
    v.cm_bf = xbc[:, M2_W + M2_GROUPS * M2_STATE:].astype(BF16)
    v.dt = v.dtT.T
    v.cs = v.csT.T
    return v


def _head_cols(hd):
    return slice(hd * HG_DK, (hd + 1) * HG_DK)


def _store_mix_hg(v, p, st, hd, o0, o1):
    cols = _head_cols(hd)
    for rows, o in ((_C0, o0), (_C1, o1)):
        st.mix_hg[rows, cols] = (_rms_rows(o) * p.hg_norm_w[:, cols] * v.gate_hg[rows, cols]).astype(BF16)


_C0 = slice(0, HG_CHUNK)
_C1 = slice(HG_CHUNK, 2 * HG_CHUNK)


def _waves(v, p, st):
    tl = v.x.shape[0]
    c0, c1 = _C0, _C1
    rc, cc = _iota2((HG_CHUNK, HG_CHUNK))
    causal0 = cc <= rc
    r1, k1 = _iota2((HG_CHUNK, tl))
    causal1 = k1 - HG_CHUNK <= r1
    v.o_inter = []
    for pr in range(HG_HEADS // 2):
        st_pair = _block_diag2(st.hg_state[2 * pr].astype(BF16), st.hg_state[2 * pr + 1].astype(BF16))
        oi = _dot_nt(v.qg_bf[:, 2 * pr * HG_DK:(2 * pr + 2) * HG_DK], st_pair)
        v.o_inter += [oi[:, 0:HG_DV], oi[:, HG_DV:2 * HG_DV]]
    att0, att1 = [], []
    for hd in range(HG_HEADS):
        cols = _head_cols(hd)
        att0.append(jnp.where(causal0, _dot_nt(v.qt_bf[c0, cols], v.kt_bf[c0, cols]), 0.0).astype(BF16))
        att1.append(jnp.where(causal1, _dot_nt(v.qt_bf[c1, cols], v.keys1_bf[:, cols]), 0.0).astype(BF16))
    for hd in range(HG_HEADS):
        cols = _head_cols(hd)
        st.hg_state[hd] = st.hg_state[hd] * v.e_tile[:, cols] + _dot(v.vT_bf[cols, :], v.kend_bf[:, cols])

    hr, hc = _iota2((M2_HEADS, M2_W))
    expand = jnp.where(hc // M2_HEADDIM == hr, 1.0, 0.0).astype(BF16)
    r2, c2 = _iota2((M2_CHUNK, M2_CHUNK))
    causal_m2 = c2 <= r2
    qr, qc = _iota2((M2_QUAD * M2_CHUNK, M2_QUAD * M2_HEADDIM))
    quad_mask = (qr // M2_CHUNK) == (qc // M2_HEADDIM)
    nck = tl // M2_CHUNK
    rows_of = lambda c: slice(c * M2_CHUNK, (c + 1) * M2_CHUNK)
    ncols_of = lambda gi: slice(gi * M2_STATE, (gi + 1) * M2_STATE)
    gcols_of = lambda gi: slice(gi * M2_GW, (gi + 1) * M2_GW)
    xs, cs, csT, dtT = v.xs, v.cs, v.csT, v.dtT
    xs_bf = xs.astype(BF16)
    cb = {(c, gi): _dot_nt(v.cm_bf[rows_of(c), ncols_of(gi)], v.bm_bf[rows_of(c), ncols_of(gi)])
          for c in range(nck) for gi in range(M2_GROUPS)}
    ecs_x, xdec_bf = [], []
    for c in range(nck):
        cs_c = cs[rows_of(c), :]
        cs_last = cs_c[M2_CHUNK - 1:M2_CHUNK, :]
        ecs_x.append(_dot(jnp.exp2(cs_c).astype(BF16), expand))
        dec_x = _dot((jnp.exp2(cs_last - cs_c) * v.dt[rows_of(c), :]).astype(BF16), expand)
        xdec_bf.append((xs[rows_of(c), :] * dec_x).astype(BF16))
    y_off = {}
    for gi in range(M2_GROUPS):
        stT = st.m2_state[gi]
        for c in range(nck):
            cg = v.cm_bf[rows_of(c), ncols_of(gi)]
            y_off[c, gi] = _dot(cg, stT.astype(BF16)) * ecs_x[c][:, gcols_of(gi)]
            e_last_x = ecs_x[c][M2_CHUNK - 1:M2_CHUNK, gcols_of(gi)]
            stT = stT * e_last_x + _dot_tn(v.bm_bf[rows_of(c), ncols_of(gi)], xdec_bf[c][:, gcols_of(gi)])
        st.m2_state[gi] = stT

    for hd in range(HG_HEADS):
        cols = _head_cols(hd)
        vT = v.vT_bf[cols, :]
        _store_mix_hg(v, p, st, hd, v.o_inter[hd][c0, :] + _dot_nt(att0[hd], vT[:, c0]),
                      v.o_inter[hd][c1, :] + _dot_nt(att1[hd], vT))

    y_rows = []
    for c in range(nck):
        rows = rows_of(c)
        y_quads = []
        for gi in range(M2_GROUPS):
            for qd in range(M2_HPG // M2_QUAD):
                ws = []
                for hh in range(M2_QUAD):
                    hidx = gi * M2_HPG + qd * M2_QUAD + hh
                    seg = cs[rows, hidx:hidx + 1] - csT[hidx:hidx + 1, rows]
                    lm = jnp.exp2(jnp.where(causal_m2, seg, -jnp.inf))
                    ws.append(((cb[c, gi] * lm) * dtT[hidx:hidx + 1, rows]).astype(BF16))
                q0 = (gi * M2_HPG + qd * M2_QUAD) * M2_HEADDIM
                xq = xs_bf[rows, q0:q0 + M2_QUAD * M2_HEADDIM]
                x_bd = jnp.where(quad_mask, jnp.concatenate([xq] * M2_QUAD, axis=0), jnp.zeros_like(quad_mask, BF16))
                y_diag = _dot(jnp.concatenate(ws, axis=1), x_bd)
                o0 = qd * M2_QUAD * M2_HEADDIM
                y_quads.append(y_diag + y_off[c, gi][:, o0:o0 + M2_QUAD * M2_HEADDIM])
        y_rows.append(jnp.concatenate(y_quads, axis=1))

    y = jnp.concatenate(y_rows, axis=0) + p.d_skip[...] * xs
    y = y * v.gate_m2
    mix_m2 = (jnp.concatenate([_rms_rows(y[:, gi * M2_GW:(gi + 1) * M2_GW]) for gi in range(M2_GROUPS)], axis=-1)
              * p.m2_norm_w[...]).astype(BF16)
    v.out_m2 = _dot(mix_m2, p.w_out[HG_W:HG_W + M2_W, :])


def _redo_heads_exact(v, p, st):
    c0, c1 = _C0, _C1
    same = _hg_same_chunk_exact(v.q, v.k, v.b2, v.x.shape[0])
    for hd in range(HG_HEADS):
        cols = _head_cols(hd)
        vT = v.vT_bf[cols, :]
        cross = _dot_nt(v.qt_bf[c1, cols], v.khat0_bf[:, cols])
        o0 = v.o_inter[hd][c0, :] + _dot_nt(same[hd][c0, c0].astype(BF16), vT[:, c0])
        o1 = (v.o_inter[hd][c1, :] + _dot_nt(cross.astype(BF16), vT[:, c0])
              + _dot_nt(same[hd][c1, c1].astype(BF16), vT[:, c1]))
        _store_mix_hg(v, p, st, hd, o0, o1)


def _tail(v, p, st, apply_final_norm):
    xr = v.x + (_dot(st.mix_hg[...], p.w_out[0:HG_W, :]) + v.out_m2)
    if apply_final_norm:
        ms2 = jnp.mean(xr * xr, axis=-1, keepdims=True)
        xr = (xr * lax.rsqrt(ms2 + EPS)) * p.final_norm_w[...]
    return xr


def _cast_rows_to_bf16(src_hbm, dst_ref, stage_ref, sem):
    rc = stage_ref.shape[1]
    nchunk = src_hbm.shape[0] // rc
    assert nchunk * rc == src_hbm.shape[0]

    def chunk_copy(i):
        return pltpu.make_async_copy(src_hbm.at[pl.ds(i * rc, rc), :], stage_ref.at[i % 2], sem.at[i % 2])

    chunk_copy(0).start()
    for i in range(nchunk):
        if i + 1 < nchunk:
            chunk_copy(i + 1).start()
        chunk_copy(i).wait()
        dst_ref[i * rc:(i + 1) * rc, :] = stage_ref[i % 2].astype(BF16)


def _layer_kernel(x_ref, norm_w_ref, w_in_hbm, w_iT_ref, wdtT_hi_ref, wdtT_lo_ref,
                  lb_logits_ref, hg_norm_w_ref, conv_w_ref, conv_b_ref,
                  dt_bias_col_ref, a_log_col_ref,
                  d_skip_ref, m2_norm_w_ref, w_out_hbm, final_norm_w_ref,
                  o_ref,
                  hg_state_ref, m2_state_ref, ubuf_ref, mix_hg_ref,
                  w_main_ref, w_out_ref, stage_in_ref, stage_out_ref, sem_in, sem_out,
                  *, layer, apply_final_norm):
    nstream, tl = x_ref.shape[0], x_ref.shape[1]
    assert tl == 2 * HG_CHUNK and tl % M2_CHUNK == 0
    p = _Vals(norm_w=norm_w_ref, w_main=w_main_ref, w_iT=w_iT_ref, wdtT_hi=wdtT_hi_ref, wdtT_lo=wdtT_lo_ref,
              lb_logits=lb_logits_ref, hg_norm_w=hg_norm_w_ref, conv_w=conv_w_ref, conv_b=conv_b_ref,
              dt_bias_col=dt_bias_col_ref, a_log_col=a_log_col_ref, d_skip=d_skip_ref,
              m2_norm_w=m2_norm_w_ref, w_out=w_out_ref, final_norm_w=final_norm_w_ref)
    sts = [_Vals(hg_state=hg_state_ref.at[s], m2_state=m2_state_ref.at[s], ubuf=ubuf_ref.at[s],
                 mix_hg=mix_hg_ref.at[s]) for s in range(nstream)]

    @pl.when((pl.program_id(0) == 0) & (pl.program_id(1) == 0))
    def _():
        _cast_rows_to_bf16(w_in_hbm, w_main_ref, stage_in_ref, sem_in)
        _cast_rows_to_bf16(w_out_hbm, w_out_ref, stage_out_ref, sem_out)

    @pl.when(pl.program_id(1) == 0)
    def _():
        hg_state_ref[...] = jnp.zeros_like(hg_state_ref)
        m2_state_ref[...] = jnp.zeros_like(m2_state_ref)
        for s in range(nstream):
            ubuf_ref[s, 0:CONV_PAD, :] = jnp.zeros((CONV_PAD, M2_CONV_DIM), F32)

    vals = []
    for s in range(nstream):
        vals.append(_front(x_ref[s], p, sts[s], layer))
        _waves(vals[s], p, sts[s])

    for s in range(nstream):
        @pl.when(jnp.logical_not(vals[s].same_chunk_safe))
        def _(s=s):
            _redo_heads_exact(vals[s], p, sts[s])

    for s in range(nstream):
        o_ref[s] = _tail(vals[s], p, sts[s], apply_final_norm).astype(o_ref.dtype)


def _const_spec(shape):
    nd = len(shape)
    return pl.BlockSpec(shape, lambda b, l: (0,) * nd, pipeline_mode=pl.Buffered(1))


def _layer(x, layer, depth, norm_w, w_in, hg_lb_logits, hg_norm_w, conv_w, conv_b, dt_bias, a_log,
           d_skip, m2_norm_w, w_out, final_norm_w):
    bsz, seq, d = x.shape
    assert d == D_MODEL and seq % SEQ_TILE == 0 and bsz % STREAMS == 0
    assert w_in.shape == (D_MODEL, MAIN_COLS + M2_HEADS)
    tl, ns = SEQ_TILE, STREAMS

    wdt = w_in[:, MAIN_COLS:]
    wdt_hi = wdt.astype(BF16)
    wdt_lo = (wdt - wdt_hi.astype(F32)).astype(BF16)
    row = lambda t: t.reshape(1, -1).astype(F32)
    col = lambda t: t.reshape(-1, 1).astype(F32)

    operands = (
        x, row(norm_w), w_in.astype(F32), w_in[:, OFF_I:OFF_I + HG_W].T.astype(BF16), wdt_hi.T, wdt_lo.T,
        hg_lb_logits.astype(F32), row(hg_norm_w), conv_w.astype(F32), row(conv_b),
        col(dt_bias), col(a_log),
        row(jnp.repeat(d_skip, M2_HEADDIM)), row(m2_norm_w), w_out.astype(F32), row(final_norm_w),
    )
    hbm_operands = (2, 14)
    in_specs = [pl.BlockSpec((ns, tl, d), lambda b, l: (b, l, 0))]
    in_specs += [pl.BlockSpec(memory_space=pl.ANY) if i in hbm_operands else _const_spec(op.shape)
                 for i, op in enumerate(operands) if i > 0]

    kern = functools.partial(_layer_kernel, layer=layer, apply_final_norm=(layer == depth - 1))
    return pl.pallas_call(
        kern,
        grid=(bsz // ns, seq // tl),
        in_specs=in_specs,
        out_specs=pl.BlockSpec((ns, tl, d), lambda b, l: (b, l, 0)),
        out_shape=jax.ShapeDtypeStruct(x.shape, x.dtype),
        scratch_shapes=[
            pltpu.VMEM((ns, HG_HEADS, HG_DV, HG_DK), F32),
            pltpu.VMEM((ns, M2_GROUPS, M2_STATE, M2_GW), F32),
            pltpu.VMEM((ns, CONV_PAD + tl, M2_CONV_DIM), F32),
            pltpu.VMEM((ns, tl, HG_W), BF16),
            pltpu.VMEM(w_in.shape, BF16),
            pltpu.VMEM(w_out.shape, BF16),
            pltpu.VMEM((2, W_IN_STAGE_ROWS, w_in.shape[1]), F32),
            pltpu.VMEM((2, W_OUT_STAGE_ROWS, w_out.shape[1]), F32),
            pltpu.SemaphoreType.DMA((2,)),
            pltpu.SemaphoreType.DMA((2,)),
        ],
        compiler_params=pltpu.CompilerParams(
            dimension_semantics=("arbitrary", "arbitrary"),
            vmem_limit_bytes=VMEM_LIMIT_BYTES,
        ),
        name="hybrid_layer",
    )(*operands)


@jax.jit
def kernel(x, norm_w, w_in, hg_lb_logits, hg_norm_w, m2_conv_w, m2_conv_b, m2_dt_bias, m2_a_log,
           m2_d_skip, m2_norm_w, w_out, final_norm_w):
    depth = w_in.shape[0]
    for l in range(depth):
        x = _layer(x, l, depth, norm_w[l], w_in[l], hg_lb_logits, hg_norm_w[l], m2_conv_w[l],
                   m2_conv_b[l], m2_dt_bias[l], m2_a_log[l], m2_d_skip[l], m2_norm_w[l], w_out[l],
                   final_norm_w)
    return x
```

```python
import functools
import math

import jax
import jax.numpy as jnp
from jax import lax
from jax.experimental import pallas as pl
from jax.experimental.pallas import tpu as pltpu

F32 = jnp.float32
BF16 = jnp.bfloat16

EPS = 1e-6
LOG2E = math.log2(math.e)
D_MODEL = 1024
HG_HEADS = 8
HG_DK = 128
HG_DV = 128
HG_W = HG_HEADS * HG_DK
HG_CHUNK = 128
M2_HEADDIM = 64
M2_W = 1024
M2_HEADS = M2_W // M2_HEADDIM
M2_GROUPS = 2
M2_STATE = 128
M2_CONV = 4
M2_CHUNK = 128
M2_CONV_DIM = M2_W + 2 * M2_GROUPS * M2_STATE
M2_GW = M2_W // M2_GROUPS
M2_HPG = M2_HEADS // M2_GROUPS
M2_QUAD = 4

OFF_Q = 0
OFF_F = OFF_Q + HG_W
OFF_I = OFF_F + HG_W
OFF_G = OFF_I + HG_W
OFF_Z = OFF_G + HG_W
OFF_XBC = OFF_Z + M2_W
OFF_DT = OFF_XBC + M2_CONV_DIM
MAIN_COLS = OFF_DT

SEQ_TILE = 2 * HG_CHUNK
STREAMS = 1
CONV_PAD = 8
W_IN_STAGE_ROWS = 128
W_OUT_STAGE_ROWS = 256
VMEM_LIMIT_BYTES = 58 * 1024 * 1024
HG_SAFE_LOG2_DECAY = -108.0


def _dot(a, b):
    return jnp.dot(a, b, preferred_element_type=F32)


def _dot_nt(a, b):
    return lax.dot_general(a, b, (((1,), (1,)), ((), ())), preferred_element_type=F32)


def _dot_tn(a, b):
    return lax.dot_general(a, b, (((0,), (0,)), ((), ())), preferred_element_type=F32)


def _split3(x):
    hi = x.astype(BF16)
    r1 = x - hi.astype(F32)
    mid = r1.astype(BF16)
    lo = (r1 - mid.astype(F32)).astype(BF16)
    return hi, mid, lo


def _dot_exact_left(mat_bf16, x):
    hi, mid, lo = _split3(x)
    return (_dot(mat_bf16, lo) + _dot(mat_bf16, mid)) + _dot(mat_bf16, hi)


def _dot_exact_right(x, mat_bf16):
    hi, mid, lo = _split3(x)
    return (_dot(lo, mat_bf16) + _dot(mid, mat_bf16)) + _dot(hi, mat_bf16)


def _sigmoid(x):
    return 1.0 / (1.0 + jnp.exp2(x * (-LOG2E)))


def _silu(x):
    return x * _sigmoid(x)


def _softplus(x):
    return jnp.maximum(x, 0.0) + jnp.log(1.0 + jnp.exp(-jnp.abs(x)))


def _iota2(shape):
    return lax.broadcasted_iota(jnp.int32, shape, 0), lax.broadcasted_iota(jnp.int32, shape, 1)


def _block_tril(n, block, upper=False):
    r, c = _iota2((n, n))
    same = (r // block) == (c // block)
    tri = (r <= c) if upper else (c <= r)
    return jnp.where(same & tri, 1.0, 0.0).astype(BF16)


def _block_diag2(a, b):
    z = jnp.zeros_like(a)
    return jnp.concatenate([jnp.concatenate([a, z], axis=1), jnp.concatenate([z, b], axis=1)], axis=0)


def _chunk_cumsum(x, chunk):
    n, w = x.shape
    sub = lax.broadcasted_iota(jnp.int32, (8, w), 0)
    out = []
    run = None
    for g in range(n // 8):
        blk = x[8 * g:8 * g + 8, :]
        for d in (1, 2, 4):
            blk = blk + jnp.where(sub >= d, pltpu.roll(blk, d, axis=0), 0.0)
        if (8 * g) % chunk != 0:
            blk = blk + run
        run = jnp.broadcast_to(blk[7:8, :], (8, w))
        out.append(blk)
    return jnp.concatenate(out, axis=0)


def _rms_rows(o):
    return o * lax.rsqrt(jnp.mean(o * o, axis=-1, keepdims=True) + EPS)


def _hg_same_chunk_exact(q, k, b2, tl):
    r, c = _iota2((tl, tl))
    q_bf = q.astype(BF16)
    k_bf = k.astype(BF16)
    atts = [jnp.where(r == c, _dot_nt(q_bf[:, hd * HG_DK:(hd + 1) * HG_DK], k_bf[:, hd * HG_DK:(hd + 1) * HG_DK]), 0.0)
            for hd in range(HG_HEADS)]
    row = lax.broadcasted_iota(jnp.int32, (tl, 1), 0)
    h = HG_CHUNK // 2
    while h >= 1:
        mid = (r // (2 * h)) * (2 * h) + (h - 1)
        sel = jnp.where(c == mid, 1.0, 0.0).astype(BF16)
        ref = _dot_exact_left(sel, b2)
        e = jnp.exp2(-jnp.abs(b2 - ref))
        is_q = (row % (2 * h)) >= h
        zq = jnp.where(is_q, q * e, 0.0).astype(BF16)
        zk = jnp.where(is_q, 0.0, k * e).astype(BF16)
        pair = ((r // (2 * h)) == (c // (2 * h))) & ((r % (2 * h)) >= h) & ((c % (2 * h)) < h)
        for hd in range(HG_HEADS):
            cols = slice(hd * HG_DK, (hd + 1) * HG_DK)
            atts[hd] = atts[hd] + jnp.where(pair, _dot_nt(zq[:, cols], zk[:, cols]), 0.0)
        h //= 2
    return atts


class _Vals(dict):
    __getattr__ = dict.__getitem__
    __setattr__ = dict.__setitem__


def _front(x, p, st, layer):
    tl = x.shape[0]
    v = _Vals(x=x)
    ms = jnp.mean(x * x, axis=-1, keepdims=True)
    h = (x * lax.rsqrt(ms + EPS)) * p.norm_w[...]
    h_bf = h.astype(BF16)
    h_lo = (h - h_bf.astype(F32)).astype(BF16)

    def proj(off, width):
        return _dot(h_bf, p.w_main[:, off:off + width])

    fl = proj(OFF_F, HG_W)
    wdtT_hi = p.wdtT_hi[...]
    wdtT_lo = p.wdtT_lo[...]
    dt_rawT = (_dot_nt(wdtT_hi, h_lo) + _dot_nt(wdtT_lo, h_bf)) + _dot_nt(wdtT_hi, h_bf)
    q_raw = proj(OFF_Q, HG_W)
    v.dtT = _softplus(dt_rawT + p.dt_bias_col[...])
    aT = v.dtT * (jnp.exp(p.a_log_col[...]) * (-LOG2E))
    u = proj(OFF_XBC, M2_CONV_DIM)
    v.csT = _dot_exact_right(aT, _block_tril(tl, M2_CHUNK, upper=True))
    v.vT_bf = _dot_nt(p.w_iT[...], h_bf).astype(BF16)
    v.gate_hg = _silu(proj(OFF_G, HG_W)).astype(BF16)
    v.gate_m2 = _silu(proj(OFF_Z, M2_W)).astype(BF16)

    logits = p.lb_logits[...]
    mx = jnp.max(logits, axis=0, keepdims=True)
    ez = jnp.exp(logits - mx)
    lb = jnp.sum(ez[0:layer + 1], axis=0, keepdims=True) / jnp.sum(ez, axis=0, keepdims=True)
    one_m_lb = 1.0 - lb
    sg = _sigmoid(fl)
    f = lb + one_m_lb * sg
    k = one_m_lb * (1.0 - sg)
    b2 = _chunk_cumsum(jnp.log2(f), HG_CHUNK)
    e = jnp.exp2(b2)
    q = _silu(q_raw)
    qt = q * e
    v.q, v.k, v.b2 = q, k, b2
    v.qt_bf = qt.astype(BF16)
    v.kt_bf = (k / e).astype(BF16)
    c0, c1 = _C0, _C1
    tot0 = b2[HG_CHUNK - 1:HG_CHUNK, :]
    tot1 = b2[2 * HG_CHUNK - 1:2 * HG_CHUNK, :]
    khat0 = k[c0, :] * jnp.exp2(tot0 - b2[c0, :])
    khat1 = k[c1, :] * jnp.exp2(tot1 - b2[c1, :])
    v.khat0_bf = khat0.astype(BF16)
    v.kend_bf = jnp.concatenate([(khat0 * jnp.exp2(tot1)).astype(BF16), khat1.astype(BF16)], axis=0)
    v.qg_bf = jnp.concatenate([v.qt_bf[c0, :], (qt[c1, :] * jnp.exp2(tot0)).astype(BF16)], axis=0)
    v.keys1_bf = jnp.concatenate([v.khat0_bf, v.kt_bf[c1, :]], axis=0)
    v.e_tile = jnp.exp2(tot0 + tot1)
    v.same_chunk_safe = jnp.min(jnp.minimum(tot0, tot1)) >= HG_SAFE_LOG2_DECAY

    st.ubuf[CONV_PAD:CONV_PAD + tl, :] = u
    conv = u * p.conv_w[M2_CONV - 1:M2_CONV, :]
    for j in range(M2_CONV - 1):
        sh = M2_CONV - 1 - j
        conv = conv + st.ubuf[CONV_PAD - sh:CONV_PAD - sh + tl, :] * p.conv_w[j:j + 1, :]
    st.ubuf[0:CONV_PAD, :] = st.ubuf[tl:tl + CONV_PAD, :]
    xbc = _silu(conv + p.conv_b[...])
    v.xs = xbc[:, 0:M2_W]
    v.bm_bf = xbc[:, M2_W:M2_W + M2_GROUPS * M2_STATE].astype(BF16)
    v.cm_bf = xbc[:, M2_W + M2_GROUPS * M2_STATE:].astype(BF16)
    v.dt = v.dtT.T
    v.cs = v.csT.T
    return v


def _head_cols(hd):
    return slice(hd * HG_DK, (hd + 1) * HG_DK)


def _store_mix_hg(v, p, st, hd, o0, o1):
    cols = _head_cols(hd)
    for rows, o in ((_C0, o0), (_C1, o1)):
        st.mix_hg[rows, cols] = (_rms_rows(o) * p.hg_norm_w[:, cols] * v.gate_hg[rows, cols]).astype(BF16)


_C0 = slice(0, HG_CHUNK)
_C1 = slice(HG_CHUNK, 2 * HG_CHUNK)


def _waves(v, p, st):
    tl = v.x.shape[0]
    c0, c1 = _C0, _C1
    rc, cc = _iota2((HG_CHUNK, HG_CHUNK))
    causal0 = cc <= rc
    r1, k1 = _iota2((HG_CHUNK, tl))
    causal1 = k1 - HG_CHUNK <= r1
    v.o_inter = []
    for pr in range(HG_HEADS // 2):
        st_pair = _block_diag2(st.hg_state[2 * pr].astype(BF16), st.hg_state[2 * pr + 1].astype(BF16))
        oi = _dot_nt(v.qg_bf[:, 2 * pr * HG_DK:(2 * pr + 2) * HG_DK], st_pair)
        v.o_inter += [oi[:, 0:HG_DV], oi[:, HG_DV:2 * HG_DV]]
    att0, att1 = [], []
    for hd in range(HG_HEADS):
        cols = _head_cols(hd)
        att0.append(jnp.where(causal0, _dot_nt(v.qt_bf[c0, cols], v.kt_bf[c0, cols]), 0.0).astype(BF16))
        att1.append(jnp.where(causal1, _dot_nt(v.qt_bf[c1, cols], v.keys1_bf[:, cols]), 0.0).astype(BF16))
    for hd in range(HG_HEADS):
        cols = _head_cols(hd)
        st.hg_state[hd] = st.hg_state[hd] * v.e_tile[:, cols] + _dot(v.vT_bf[cols, :], v.kend_bf[:, cols])

    hr, hc = _iota2((M2_HEADS, M2_W))
    expand = jnp.where(hc // M2_HEADDIM == hr, 1.0, 0.0).astype(BF16)
    r2, c2 = _iota2((M2_CHUNK, M2_CHUNK))
    causal_m2 = c2 <= r2
    qr, qc = _iota2((M2_QUAD * M2_CHUNK, M2_QUAD * M2_HEADDIM))
    quad_mask = (qr // M2_CHUNK) == (qc // M2_HEADDIM)
    nck = tl // M2_CHUNK
    rows_of = lambda c: slice(c * M2_CHUNK, (c + 1) * M2_CHUNK)
    ncols_of = lambda gi: slice(gi * M2_STATE, (gi + 1) * M2_STATE)
    gcols_of = lambda gi: slice(gi * M2_GW, (gi + 1) * M2_GW)
    xs, cs, csT, dtT = v.xs, v.cs, v.csT, v.dtT
    xs_bf = xs.astype(BF16)
    cb = {(c, gi): _dot_nt(v.cm_bf[rows_of(c), ncols_of(gi)], v.bm_bf[rows_of(c), ncols_of(gi)])
          for c in range(nck) for gi in range(M2_GROUPS)}
    ecs_x, xdec_bf = [], []
    for c in range(nck):
        cs_c = cs[rows_of(c), :]
        cs_last = cs_c[M2_CHUNK - 1:M2_CHUNK, :]
        ecs_x.append(_dot(jnp.exp2(cs_c).astype(BF16), expand))
        dec_x = _dot((jnp.exp2(cs_last - cs_c) * v.dt[rows_of(c), :]).astype(BF16), expand)
        xdec_bf.append((xs[rows_of(c), :] * dec_x).astype(BF16))
    y_off = {}
    for gi in range(M2_GROUPS):
        stT = st.m2_state[gi]
        for c in range(nck):
            cg = v.cm_bf[rows_of(c), ncols_of(gi)]
            y_off[c, gi] = _dot(cg, stT.astype(BF16)) * ecs_x[c][:, gcols_of(gi)]
            e_last_x = ecs_x[c][M2_CHUNK - 1:M2_CHUNK, gcols_of(gi)]
            stT = stT * e_last_x + _dot_tn(v.bm_bf[rows_of(c), ncols_of(gi)], xdec_bf[c][:, gcols_of(gi)])
        st.m2_state[gi] = stT

    for hd in range(HG_HEADS):
        cols = _head_cols(hd)
        vT = v.vT_bf[cols, :]
        _store_mix_hg(v, p, st, hd, v.o_inter[hd][c0, :] + _dot_nt(att0[hd], vT[:, c0]),
                      v.o_inter[hd][c1, :] + _dot_nt(att1[hd], vT))

    y_rows = []
    for c in range(nck):
        rows = rows_of(c)
        y_quads = []
        for gi in range(M2_GROUPS):
            for qd in range(M2_HPG // M2_QUAD):
                ws = []
                for hh in range(M2_QUAD):
                    hidx = gi * M2_HPG + qd * M2_QUAD + hh
                    seg = cs[rows, hidx:hidx + 1] - csT[hidx:hidx + 1, rows]
                    lm = jnp.exp2(jnp.where(causal_m2, seg, -jnp.inf))
                    ws.append(((cb[c, gi] * lm) * dtT[hidx:hidx + 1, rows]).astype(BF16))
                q0 = (gi * M2_HPG + qd * M2_QUAD) * M2_HEADDIM
                xq = xs_bf[rows, q0:q0 + M2_QUAD * M2_HEADDIM]
                x_bd = jnp.where(quad_mask, jnp.concatenate([xq] * M2_QUAD, axis=0), jnp.zeros_like(quad_mask, BF16))
                y_diag = _dot(jnp.concatenate(ws, axis=1), x_bd)
                o0 = qd * M2_QUAD * M2_HEADDIM
                y_quads.append(y_diag + y_off[c, gi][:, o0:o0 + M2_QUAD * M2_HEADDIM])
        y_rows.append(jnp.concatenate(y_quads, axis=1))

    y = jnp.concatenate(y_rows, axis=0) + p.d_skip[...] * xs
    y = y * v.gate_m2
    mix_m2 = (jnp.concatenate([_rms_rows(y[:, gi * M2_GW:(gi + 1) * M2_GW]) for gi in range(M2_GROUPS)], axis=-1)
              * p.m2_norm_w[...]).astype(BF16)
    v.out_m2 = _dot(mix_m2, p.w_out[HG_W:HG_W + M2_W, :])


def _redo_heads_exact(v, p, st):
    c0, c1 = _C0, _C1
    same = _hg_same_chunk_exact(v.q, v.k, v.b2, v.x.shape[0])
    for hd in range(HG_HEADS):
        cols = _head_cols(hd)
        vT = v.vT_bf[cols, :]
        cross = _dot_nt(v.qt_bf[c1, cols], v.khat0_bf[:, cols])
        o0 = v.o_inter[hd][c0, :] + _dot_nt(same[hd][c0, c0].astype(BF16), vT[:, c0])
        o1 = (v.o_inter[hd][c1, :] + _dot_nt(cross.astype(BF16), vT[:, c0])
              + _dot_nt(same[hd][c1, c1].astype(BF16), vT[:, c1]))
        _store_mix_hg(v, p, st, hd, o0, o1)


def _tail(v, p, st, apply_final_norm):
    xr = v.x + (_dot(st.mix_hg[...], p.w_out[0:HG_W, :]) + v.out_m2)
    if apply_final_norm:
        ms2 = jnp.mean(xr * xr, axis=-1, keepdims=True)
        xr = (xr * lax.rsqrt(ms2 + EPS)) * p.final_norm_w[...]
    return xr


def _stream_rows(src_hbm, nrows, stage_ref, sem, consume):
    rc = stage_ref.shape[1]
    nchunk = nrows // rc
    assert nchunk * rc == nrows

    def chunk_copy(i):
        return pltpu.make_async_copy(src_hbm.at[pl.ds(i * rc, rc), :], stage_ref.at[i % 2], sem.at[i % 2])

    chunk_copy(0).start()
    for i in range(nchunk):
        if i + 1 < nchunk:
            chunk_copy(i + 1).start()
        chunk_copy(i).wait()
        consume(i, stage_ref[i % 2])


def _load_weights(w_inT_hbm, w_out_hbm, w_main_ref, w_iT_ref, w_out_ref, stage_in_ref, stage_out_ref, sem_in, sem_out):
    rc = stage_in_ref.shape[1]

    def put_in(i, chunk):
        w_main_ref[:, i * rc:(i + 1) * rc] = chunk.T.astype(BF16)
        lo = i * rc - OFF_I
        if 0 <= lo < HG_W:
            w_iT_ref[lo:lo + rc, :] = chunk.astype(BF16)

    def put_out(i, chunk):
        ro = stage_out_ref.shape[1]
        w_out_ref[i * ro:(i + 1) * ro, :] = chunk.astype(BF16)

    _stream_rows(w_inT_hbm, MAIN_COLS, stage_in_ref, sem_in, put_in)
    _stream_rows(w_out_hbm, w_out_hbm.shape[0], stage_out_ref, sem_out, put_out)


def _layer_kernel(x_ref, norm_w_ref, w_inT_hbm, wdtT_hi_ref, wdtT_lo_ref,
                  lb_logits_ref, hg_norm_w_ref, conv_w_ref, conv_b_ref,
                  dt_bias_col_ref, a_log_col_ref,
                  d_skip_ref, m2_norm_w_ref, w_out_hbm, final_norm_w_ref,
                  o_ref,
                  hg_state_ref, m2_state_ref, ubuf_ref, mix_hg_ref,
                  w_main_ref, w_iT_ref, w_out_ref, stage_in_ref, stage_out_ref, sem_in, sem_out,
                  *, layer, apply_final_norm):
    nstream, tl = x_ref.shape[0], x_ref.shape[1]
    assert tl == 2 * HG_CHUNK and tl % M2_CHUNK == 0
    p = _Vals(norm_w=norm_w_ref, w_main=w_main_ref, w_iT=w_iT_ref, wdtT_hi=wdtT_hi_ref, wdtT_lo=wdtT_lo_ref,
              lb_logits=lb_logits_ref, hg_norm_w=hg_norm_w_ref, conv_w=conv_w_ref, conv_b=conv_b_ref,
              dt_bias_col=dt_bias_col_ref, a_log_col=a_log_col_ref, d_skip=d_skip_ref,
              m2_norm_w=m2_norm_w_ref, w_out=w_out_ref, final_norm_w=final_norm_w_ref)
    sts = [_Vals(hg_state=hg_state_ref.at[s], m2_state=m2_state_ref.at[s], ubuf=ubuf_ref.at[s],
                 mix_hg=mix_hg_ref.at[s]) for s in range(nstream)]

    @pl.when((pl.program_id(0) == 0) & (pl.program_id(1) == 0))
    def _():
        _load_weights(w_inT_hbm, w_out_hbm, w_main_ref, w_iT_ref, w_out_ref,
                      stage_in_ref, stage_out_ref, sem_in, sem_out)

    @pl.when(pl.program_id(1) == 0)
    def _():
        hg_state_ref[...] = jnp.zeros_like(hg_state_ref)
        m2_state_ref[...] = jnp.zeros_like(m2_state_ref)
        for s in range(nstream):
            ubuf_ref[s, 0:CONV_PAD, :] = jnp.zeros((CONV_PAD, M2_CONV_DIM), F32)

    vals = []
    for s in range(nstream):
        vals.append(_front(x_ref[s], p, sts[s], layer))
        _waves(vals[s], p, sts[s])

    for s in range(nstream):
        @pl.when(jnp.logical_not(vals[s].same_chunk_safe))
        def _(s=s):
            _redo_heads_exact(vals[s], p, sts[s])

    for s in range(nstream):
        o_ref[s] = _tail(vals[s], p, sts[s], apply_final_norm).astype(o_ref.dtype)


def _const_spec(shape):
    nd = len(shape)
    return pl.BlockSpec(shape, lambda b, l: (0,) * nd, pipeline_mode=pl.Buffered(1))


def _layer(x, layer, depth, norm_w, w_in, hg_lb_logits, hg_norm_w, conv_w, conv_b, dt_bias, a_log,
           d_skip, m2_norm_w, w_out, final_norm_w):
    bsz, seq, d = x.shape
    assert d == D_MODEL and seq % SEQ_TILE == 0 and bsz % STREAMS == 0
    assert w_in.shape == (D_MODEL, MAIN_COLS + M2_HEADS)
    tl, ns = SEQ_TILE, STREAMS

    wdt = w_in[:, MAIN_COLS:]
    wdt_hi = wdt.astype(BF16)
    wdt_lo = (wdt - wdt_hi.astype(F32)).astype(BF16)
    row = lambda t: t.reshape(1, -1).astype(F32)
    col = lambda t: t.reshape(-1, 1).astype(F32)

    operands = (
        x, row(norm_w), w_in.astype(F32).T, wdt_hi.T, wdt_lo.T,
        hg_lb_logits.astype(F32), row(hg_norm_w), conv_w.astype(F32), row(conv_b),
        col(dt_bias), col(a_log),
        row(jnp.repeat(d_skip, M2_HEADDIM)), row(m2_norm_w), w_out.astype(F32), row(final_norm_w),
    )
    hbm_operands = (2, 13)
    in_specs = [pl.BlockSpec((ns, tl, d), lambda b, l: (b, l, 0))]
    in_specs += [pl.BlockSpec(memory_space=pl.ANY) if i in hbm_operands else _const_spec(op.shape)
                 for i, op in enumerate(operands) if i > 0]

    kern = functools.partial(_layer_kernel, layer=layer, apply_final_norm=(layer == depth - 1))
    return pl.pallas_call(
        kern,
        grid=(bsz // ns, seq // tl),
        in_specs=in_specs,
        out_specs=pl.BlockSpec((ns, tl, d), lambda b, l: (b, l, 0)),
        out_shape=jax.ShapeDtypeStruct(x.shape, x.dtype),
        scratch_shapes=[
            pltpu.VMEM((ns, HG_HEADS, HG_DV, HG_DK), F32),
            pltpu.VMEM((ns, M2_GROUPS, M2_STATE, M2_GW), F32),
            pltpu.VMEM((ns, CONV_PAD + tl, M2_CONV_DIM), F32),
            pltpu.VMEM((ns, tl, HG_W), BF16),
            pltpu.VMEM((D_MODEL, MAIN_COLS), BF16),
            pltpu.VMEM((HG_W, D_MODEL), BF16),
            pltpu.VMEM(w_out.shape, BF16),
            pltpu.VMEM((2, W_IN_STAGE_ROWS, D_MODEL), F32),
            pltpu.VMEM((2, W_OUT_STAGE_ROWS, w_out.shape[1]), F32),
            pltpu.SemaphoreType.DMA((2,)),
            pltpu.SemaphoreType.DMA((2,)),
        ],
        compiler_params=pltpu.CompilerParams(
            dimension_semantics=("arbitrary", "arbitrary"),
            vmem_limit_bytes=VMEM_LIMIT_BYTES,
        ),
        name="hybrid_layer",
    )(*operands)


@jax.jit
def kernel(x, norm_w, w_in, hg_lb_logits, hg_norm_w, m2_conv_w, m2_conv_b, m2_dt_bias, m2_a_log,
           m2_d_skip, m2_norm_w, w_out, final_norm_w):
    depth = w_in.shape[0]
    for l in range(depth):
        x = _layer(x, l, depth, norm_w[l], w_in[l], hg_lb_logits, hg_norm_w[l], m2_conv_w[l],
                   m2_conv_b[l], m2_dt_bias[l], m2_a_log[l], m2_d_skip[l], m2_norm_w[l], w_out[l],
                   final_norm_w)
    return x
```

```python
import functools
import math

import jax
import jax.numpy as jnp
from jax import lax
from jax.experimental import pallas as pl
from jax.experimental.pallas import tpu as pltpu

F32 = jnp.float32
BF16 = jnp.bfloat16

EPS = 1e-6
LOG2E = math.log2(math.e)
D_MODEL = 1024
HG_HEADS = 8
HG_DK = 128
HG_DV = 128
HG_W = HG_HEADS * HG_DK
HG_CHUNK = 128
M2_HEADDIM = 64
M2_W = 1024
M2_HEADS = M2_W // M2_HEADDIM
M2_GROUPS = 2
M2_STATE = 128
M2_CONV = 4
M2_CHUNK = 128
M2_CONV_DIM = M2_W + 2 * M2_GROUPS * M2_STATE
M2_GW = M2_W // M2_GROUPS
M2_HPG = M2_HEADS // M2_GROUPS
M2_QUAD = 4

OFF_Q = 0
OFF_F = OFF_Q + HG_W
OFF_I = OFF_F + HG_W
OFF_G = OFF_I + HG_W
OFF_Z = OFF_G + HG_W
OFF_XBC = OFF_Z + M2_W
OFF_DT = OFF_XBC + M2_CONV_DIM
MAIN_COLS = OFF_DT

SEQ_TILE = 2 * HG_CHUNK
STREAMS = 1
CONV_PAD = 8
W_IN_STAGE_ROWS = 128
W_OUT_STAGE_ROWS = 256
VMEM_LIMIT_BYTES = 58 * 1024 * 1024
HG_SAFE_LOG2_DECAY = -108.0


def _dot(a, b):
    return jnp.dot(a, b, preferred_element_type=F32)


def _dot_nt(a, b):
    return lax.dot_general(a, b, (((1,), (1,)), ((), ())), preferred_element_type=F32)


def _dot_tn(a, b):
    return lax.dot_general(a, b, (((0,), (0,)), ((), ())), preferred_element_type=F32)


def _split3(x):
    hi = x.astype(BF16)
    r1 = x - hi.astype(F32)
    mid = r1.astype(BF16)
    lo = (r1 - mid.astype(F32)).astype(BF16)
    return hi, mid, lo


def _dot_exact_left(mat_bf16, x):
    hi, mid, lo = _split3(x)
    return (_dot(mat_bf16, lo) + _dot(mat_bf16, mid)) + _dot(mat_bf16, hi)


def _dot_exact_right(x, mat_bf16):
    hi, mid, lo = _split3(x)
    return (_dot(lo, mat_bf16) + _dot(mid, mat_bf16)) + _dot(hi, mat_bf16)


def _sigmoid(x):
    return 1.0 / (1.0 + jnp.exp2(x * (-LOG2E)))


def _silu(x):
    return x * _sigmoid(x)


def _softplus(x):
    return jnp.maximum(x, 0.0) + jnp.log(1.0 + jnp.exp(-jnp.abs(x)))


def _iota2(shape):
    return lax.broadcasted_iota(jnp.int32, shape, 0), lax.broadcasted_iota(jnp.int32, shape, 1)


def _block_tril(n, block, upper=False):
    r, c = _iota2((n, n))
    same = (r // block) == (c // block)
    tri = (r <= c) if upper else (c <= r)
    return jnp.where(same & tri, 1.0, 0.0).astype(BF16)


def _block_diag2(a, b):
    z = jnp.zeros_like(a)
    return jnp.concatenate([jnp.concatenate([a, z], axis=1), jnp.concatenate([z, b], axis=1)], axis=0)


def _chunk_cumsum(x, chunk):
    n, w = x.shape
    sub = lax.broadcasted_iota(jnp.int32, (8, w), 0)
    out = []
    run = None
    for g in range(n // 8):
        blk = x[8 * g:8 * g + 8, :]
        for d in (1, 2, 4):
            blk = blk + jnp.where(sub >= d, pltpu.roll(blk, d, axis=0), 0.0)
        if (8 * g) % chunk != 0:
            blk = blk + run
        run = jnp.broadcast_to(blk[7:8, :], (8, w))
        out.append(blk)
    return jnp.concatenate(out, axis=0)


def _rms_rows(o):
    return o * lax.rsqrt(jnp.mean(o * o, axis=-1, keepdims=True) + EPS)


def _hg_same_chunk_exact(q, k, b2, tl):
    r, c = _iota2((tl, tl))
    q_bf = q.astype(BF16)
    k_bf = k.astype(BF16)
    atts = [jnp.where(r == c, _dot_nt(q_bf[:, hd * HG_DK:(hd + 1) * HG_DK], k_bf[:, hd * HG_DK:(hd + 1) * HG_DK]), 0.0)
            for hd in range(HG_HEADS)]
    row = lax.broadcasted_iota(jnp.int32, (tl, 1), 0)
    h = HG_CHUNK // 2
    while h >= 1:
        mid = (r // (2 * h)) * (2 * h) + (h - 1)
        sel = jnp.where(c == mid, 1.0, 0.0).astype(BF16)
        ref = _dot_exact_left(sel, b2)
        e = jnp.exp2(-jnp.abs(b2 - ref))
        is_q = (row % (2 * h)) >= h
        zq = jnp.where(is_q, q * e, 0.0).astype(BF16)
        zk = jnp.where(is_q, 0.0, k * e).astype(BF16)
        pair = ((r // (2 * h)) == (c // (2 * h))) & ((r % (2 * h)) >= h) & ((c % (2 * h)) < h)
        for hd in range(HG_HEADS):
            cols = slice(hd * HG_DK, (hd + 1) * HG_DK)
            atts[hd] = atts[hd] + jnp.where(pair, _dot_nt(zq[:, cols], zk[:, cols]), 0.0)
        h //= 2
    return atts


class _Vals(dict):
    __getattr__ = dict.__getitem__
    __setattr__ = dict.__setitem__


def _front(x, p, st, layer):
    tl = x.shape[0]
    v = _Vals(x=x)
    ms = jnp.mean(x * x, axis=-1, keepdims=True)
    h = (x * lax.rsqrt(ms + EPS)) * p.norm_w[...]
    h_bf = h.astype(BF16)
    h_lo = (h - h_bf.astype(F32)).astype(BF16)

    def proj(off, width):
        return _dot(h_bf, p.w_main[:, off:off + width])

    fl = proj(OFF_F, HG_W)
    wdtT_hi = p.wdtT_hi[...]
    wdtT_lo = p.wdtT_lo[...]
    dt_rawT = (_dot_nt(wdtT_hi, h_lo) + _dot_nt(wdtT_lo, h_bf)) + _dot_nt(wdtT_hi, h_bf)
    q_raw = proj(OFF_Q, HG_W)
    v.dtT = _softplus(dt_rawT + p.dt_bias_col[...])
    aT = v.dtT * (jnp.exp(p.a_log_col[...]) * (-LOG2E))
    u = proj(OFF_XBC, M2_CONV_DIM)
    v.csT = _dot_exact_right(aT, _block_tril(tl, M2_CHUNK, upper=True))
    v.v_bf = proj(OFF_I, HG_W).astype(BF16)
    v.gate_hg = _silu(proj(OFF_G, HG_W)).astype(BF16)
    v.gate_m2 = _silu(proj(OFF_Z, M2_W)).astype(BF16)

    logits = p.lb_logits[...]
    mx = jnp.max(logits, axis=0, keepdims=True)
    ez = jnp.exp(logits - mx)
    lb = jnp.sum(ez[0:layer + 1], axis=0, keepdims=True) / jnp.sum(ez, axis=0, keepdims=True)
    one_m_lb = 1.0 - lb
    sg = _sigmoid(fl)
    f = lb + one_m_lb * sg
    k = one_m_lb * (1.0 - sg)
    b2 = _chunk_cumsum(jnp.log2(f), HG_CHUNK)
    e = jnp.exp2(b2)
    q = _silu(q_raw)
    qt = q * e
    v.q, v.k, v.b2 = q, k, b2
    v.qt_bf = qt.astype(BF16)
    v.kt_bf = (k / e).astype(BF16)
    c0, c1 = _C0, _C1
    tot0 = b2[HG_CHUNK - 1:HG_CHUNK, :]
    tot1 = b2[2 * HG_CHUNK - 1:2 * HG_CHUNK, :]
    khat0 = k[c0, :] * jnp.exp2(tot0 - b2[c0, :])
    khat1 = k[c1, :] * jnp.exp2(tot1 - b2[c1, :])
    v.khat0_bf = khat0.astype(BF16)
    v.kend_bf = jnp.concatenate([(khat0 * jnp.exp2(tot1)).astype(BF16), khat1.astype(BF16)], axis=0)
    v.qg_bf = jnp.concatenate([v.qt_bf[c0, :], (qt[c1, :] * jnp.exp2(tot0)).astype(BF16)], axis=0)
    v.keys1_bf = jnp.concatenate([v.khat0_bf, v.kt_bf[c1, :]], axis=0)
    v.e_tile = jnp.exp2(tot0 + tot1)
    v.same_chunk_safe = jnp.min(jnp.minimum(tot0, tot1)) >= HG_SAFE_LOG2_DECAY

    st.ubuf[CONV_PAD:CONV_PAD + tl, :] = u
    conv = u * p.conv_w[M2_CONV - 1:M2_CONV, :]
    for j in range(M2_CONV - 1):
        sh = M2_CONV - 1 - j
        conv = conv + st.ubuf[CONV_PAD - sh:CONV_PAD - sh + tl, :] * p.conv_w[j:j + 1, :]
    st.ubuf[0:CONV_PAD, :] = st.ubuf[tl:tl + CONV_PAD, :]
    xbc = _silu(conv + p.conv_b[...])
    v.xs = xbc[:, 0:M2_W]
    v.bm_bf = xbc[:, M2_W:M2_W + M2_GROUPS * M2_STATE].astype(BF16)
    v.cm_bf = xbc[:, M2_W + M2_GROUPS * M2_STATE:].astype(BF16)
    v.dt = v.dtT.T
    v.cs = v.csT.T
    return v


def _head_cols(hd):
    return slice(hd * HG_DK, (hd + 1) * HG_DK)


def _store_mix_hg(v, p, st, hd, o0, o1):
    cols = _head_cols(hd)
    for rows, o in ((_C0, o0), (_C1, o1)):
        st.mix_hg[rows, cols] = (_rms_rows(o) * p.hg_norm_w[:, cols] * v.gate_hg[rows, cols]).astype(BF16)


_C0 = slice(0, HG_CHUNK)
_C1 = slice(HG_CHUNK, 2 * HG_CHUNK)


def _waves(v, p, st):
    tl = v.x.shape[0]
    c0, c1 = _C0, _C1
    rc, cc = _iota2((HG_CHUNK, HG_CHUNK))
    causal0 = cc <= rc
    r1, k1 = _iota2((HG_CHUNK, tl))
    causal1 = k1 - HG_CHUNK <= r1
    v.o_inter = []
    for pr in range(HG_HEADS // 2):
        st_pair = _block_diag2(st.hg_state[2 * pr].astype(BF16), st.hg_state[2 * pr + 1].astype(BF16))
        oi = _dot_nt(v.qg_bf[:, 2 * pr * HG_DK:(2 * pr + 2) * HG_DK], st_pair)
        v.o_inter += [oi[:, 0:HG_DV], oi[:, HG_DV:2 * HG_DV]]
    att0, att1 = [], []
    for hd in range(HG_HEADS):
        cols = _head_cols(hd)
        att0.append(jnp.where(causal0, _dot_nt(v.qt_bf[c0, cols], v.kt_bf[c0, cols]), 0.0).astype(BF16))
        att1.append(jnp.where(causal1, _dot_nt(v.qt_bf[c1, cols], v.keys1_bf[:, cols]), 0.0).astype(BF16))
    for hd in range(HG_HEADS):
        cols = _head_cols(hd)
        st.hg_state[hd] = st.hg_state[hd] * v.e_tile[:, cols] + _dot_tn(v.v_bf[:, cols], v.kend_bf[:, cols])

    hr, hc = _iota2((M2_HEADS, M2_W))
    expand = jnp.where(hc // M2_HEADDIM == hr, 1.0, 0.0).astype(BF16)
    r2, c2 = _iota2((M2_CHUNK, M2_CHUNK))
    causal_m2 = c2 <= r2
    qr, qc = _iota2((M2_QUAD * M2_CHUNK, M2_QUAD * M2_HEADDIM))
    quad_mask = (qr // M2_CHUNK) == (qc // M2_HEADDIM)
    nck = tl // M2_CHUNK
    rows_of = lambda c: slice(c * M2_CHUNK, (c + 1) * M2_CHUNK)
    ncols_of = lambda gi: slice(gi * M2_STATE, (gi + 1) * M2_STATE)
    gcols_of = lambda gi: slice(gi * M2_GW, (gi + 1) * M2_GW)
    xs, cs, csT, dtT = v.xs, v.cs, v.csT, v.dtT
    xs_bf = xs.astype(BF16)
    cb = {(c, gi): _dot_nt(v.cm_bf[rows_of(c), ncols_of(gi)], v.bm_bf[rows_of(c), ncols_of(gi)])
          for c in range(nck) for gi in range(M2_GROUPS)}
    ecs_x, xdec_bf = [], []
    for c in range(nck):
        cs_c = cs[rows_of(c), :]
        cs_last = cs_c[M2_CHUNK - 1:M2_CHUNK, :]
        ecs_x.append(_dot(jnp.exp2(cs_c).astype(BF16), expand))
        dec_x = _dot((jnp.exp2(cs_last - cs_c) * v.dt[rows_of(c), :]).astype(BF16), expand)
        xdec_bf.append((xs[rows_of(c), :] * dec_x).astype(BF16))
    y_off = {}
    for gi in range(M2_GROUPS):
        stT = st.m2_state[gi]
        for c in range(nck):
            cg = v.cm_bf[rows_of(c), ncols_of(gi)]
            y_off[c, gi] = _dot(cg, stT.astype(BF16)) * ecs_x[c][:, gcols_of(gi)]
            e_last_x = ecs_x[c][M2_CHUNK - 1:M2_CHUNK, gcols_of(gi)]
            stT = stT * e_last_x + _dot_tn(v.bm_bf[rows_of(c), ncols_of(gi)], xdec_bf[c][:, gcols_of(gi)])
        st.m2_state[gi] = stT

    for hd in range(HG_HEADS):
        cols = _head_cols(hd)
        vh = v.v_bf[:, cols]
        _store_mix_hg(v, p, st, hd, v.o_inter[hd][c0, :] + _dot(att0[hd], vh[c0, :]),
                      v.o_inter[hd][c1, :] + _dot(att1[hd], vh))

    y_rows = []
    for c in range(nck):
        rows = rows_of(c)
        y_quads = []
        for gi in range(M2_GROUPS):
            for qd in range(M2_HPG // M2_QUAD):
                ws = []
                for hh in range(M2_QUAD):
                    hidx = gi * M2_HPG + qd * M2_QUAD + hh
                    seg = cs[rows, hidx:hidx + 1] - csT[hidx:hidx + 1, rows]
                    lm = jnp.exp2(jnp.where(causal_m2, seg, -jnp.inf))
                    ws.append(((cb[c, gi] * lm) * dtT[hidx:hidx + 1, rows]).astype(BF16))
                q0 = (gi * M2_HPG + qd * M2_QUAD) * M2_HEADDIM
                xq = xs_bf[rows, q0:q0 + M2_QUAD * M2_HEADDIM]
                x_bd = jnp.where(quad_mask, jnp.concatenate([xq] * M2_QUAD, axis=0), jnp.zeros_like(quad_mask, BF16))
                y_diag = _dot(jnp.concatenate(ws, axis=1), x_bd)
                o0 = qd * M2_QUAD * M2_HEADDIM
                y_quads.append(y_diag + y_off[c, gi][:, o0:o0 + M2_QUAD * M2_HEADDIM])
        y_rows.append(jnp.concatenate(y_quads, axis=1))

    y = jnp.concatenate(y_rows, axis=0) + p.d_skip[...] * xs
    y = y * v.gate_m2
    mix_m2 = (jnp.concatenate([_rms_rows(y[:, gi * M2_GW:(gi + 1) * M2_GW]) for gi in range(M2_GROUPS)], axis=-1)
              * p.m2_norm_w[...]).astype(BF16)
    v.out_m2 = _dot(mix_m2, p.w_out[HG_W:HG_W + M2_W, :])


def _redo_heads_exact(v, p, st):
    c0, c1 = _C0, _C1
    same = _hg_same_chunk_exact(v.q, v.k, v.b2, v.x.shape[0])
    for hd in range(HG_HEADS):
        cols = _head_cols(hd)
        vh = v.v_bf[:, cols]
        cross = _dot_nt(v.qt_bf[c1, cols], v.khat0_bf[:, cols])
        o0 = v.o_inter[hd][c0, :] + _dot(same[hd][c0, c0].astype(BF16), vh[c0, :])
        o1 = (v.o_inter[hd][c1, :] + _dot(cross.astype(BF16), vh[c0, :])
              + _dot(same[hd][c1, c1].astype(BF16), vh[c1, :]))
        _store_mix_hg(v, p, st, hd, o0, o1)


def _tail(v, p, st, apply_final_norm):
    xr = v.x + (_dot(st.mix_hg[...], p.w_out[0:HG_W, :]) + v.out_m2)
    if apply_final_norm:
        ms2 = jnp.mean(xr * xr, axis=-1, keepdims=True)
        xr = (xr * lax.rsqrt(ms2 + EPS)) * p.final_norm_w[...]
    return xr


def _stream_rows(src_hbm, nrows, stage_ref, sem, consume):
    rc = stage_ref.shape[1]
    nchunk = nrows // rc
    assert nchunk * rc == nrows

    def chunk_copy(i):
        return pltpu.make_async_copy(src_hbm.at[pl.ds(i * rc, rc), :], stage_ref.at[i % 2], sem.at[i % 2])

    chunk_copy(0).start()
    for i in range(nchunk):
        if i + 1 < nchunk:
            chunk_copy(i + 1).start()
        chunk_copy(i).wait()
        consume(i, stage_ref[i % 2])


def _load_weights(w_inT_hbm, w_out_hbm, w_main_ref, w_out_ref, stage_in_ref, stage_out_ref, sem_in, sem_out):
    rc = stage_in_ref.shape[1]

    def put_in(i, chunk):
        w_main_ref[:, i * rc:(i + 1) * rc] = chunk.T.astype(BF16)

    def put_out(i, chunk):
        ro = stage_out_ref.shape[1]
        w_out_ref[i * ro:(i + 1) * ro, :] = chunk.astype(BF16)

    _stream_rows(w_inT_hbm, MAIN_COLS, stage_in_ref, sem_in, put_in)
    _stream_rows(w_out_hbm, w_out_hbm.shape[0], stage_out_ref, sem_out, put_out)


def _layer_kernel(x_ref, norm_w_ref, w_inT_hbm, wdtT_hi_ref, wdtT_lo_ref,
                  lb_logits_ref, hg_norm_w_ref, conv_w_ref, conv_b_ref,
                  dt_bias_col_ref, a_log_col_ref,
                  d_skip_ref, m2_norm_w_ref, w_out_hbm, final_norm_w_ref,
                  o_ref,
                  hg_state_ref, m2_state_ref, ubuf_ref, mix_hg_ref,
                  w_main_ref, w_out_ref, stage_in_ref, stage_out_ref, sem_in, sem_out,
                  *, layer, apply_final_norm):
    nstream, tl = x_ref.shape[0], x_ref.shape[1]
    assert tl == 2 * HG_CHUNK and tl % M2_CHUNK == 0
    p = _Vals(norm_w=norm_w_ref, w_main=w_main_ref, wdtT_hi=wdtT_hi_ref, wdtT_lo=wdtT_lo_ref,
              lb_logits=lb_logits_ref, hg_norm_w=hg_norm_w_ref, conv_w=conv_w_ref, conv_b=conv_b_ref,
              dt_bias_col=dt_bias_col_ref, a_log_col=a_log_col_ref, d_skip=d_skip_ref,
              m2_norm_w=m2_norm_w_ref, w_out=w_out_ref, final_norm_w=final_norm_w_ref)
    sts = [_Vals(hg_state=hg_state_ref.at[s], m2_state=m2_state_ref.at[s], ubuf=ubuf_ref.at[s],
                 mix_hg=mix_hg_ref.at[s]) for s in range(nstream)]

    @pl.when((pl.program_id(0) == 0) & (pl.program_id(1) == 0))
    def _():
        _load_weights(w_inT_hbm, w_out_hbm, w_main_ref, w_out_ref,
                      stage_in_ref, stage_out_ref, sem_in, sem_out)

    @pl.when(pl.program_id(1) == 0)
    def _():
        hg_state_ref[...] = jnp.zeros_like(hg_state_ref)
        m2_state_ref[...] = jnp.zeros_like(m2_state_ref)
        for s in range(nstream):
            ubuf_ref[s, 0:CONV_PAD, :] = jnp.zeros((CONV_PAD, M2_CONV_DIM), F32)

    vals = []
    for s in range(nstream):
        vals.append(_front(x_ref[s], p, sts[s], layer))
        _waves(vals[s], p, sts[s])

    for s in range(nstream):
        @pl.when(jnp.logical_not(vals[s].same_chunk_safe))
        def _(s=s):
            _redo_heads_exact(vals[s], p, sts[s])

    for s in range(nstream):
        o_ref[s] = _tail(vals[s], p, sts[s], apply_final_norm).astype(o_ref.dtype)


def _const_spec(shape):
    nd = len(shape)
    return pl.BlockSpec(shape, lambda b, l: (0,) * nd, pipeline_mode=pl.Buffered(1))


def _layer(x, layer, depth, norm_w, w_in, hg_lb_logits, hg_norm_w, conv_w, conv_b, dt_bias, a_log,
           d_skip, m2_norm_w, w_out, final_norm_w):
    bsz, seq, d = x.shape
    assert d == D_MODEL and seq % SEQ_TILE == 0 and bsz % STREAMS == 0
    assert w_in.shape == (D_MODEL, MAIN_COLS + M2_HEADS)
    tl, ns = SEQ_TILE, STREAMS

    wdt = w_in[:, MAIN_COLS:]
    wdt_hi = wdt.astype(BF16)
    wdt_lo = (wdt - wdt_hi.astype(F32)).astype(BF16)
    row = lambda t: t.reshape(1, -1).astype(F32)
    col = lambda t: t.reshape(-1, 1).astype(F32)

    operands = (
        x, row(norm_w), w_in.astype(F32).T, wdt_hi.T, wdt_lo.T,
        hg_lb_logits.astype(F32), row(hg_norm_w), conv_w.astype(F32), row(conv_b),
        col(dt_bias), col(a_log),
        row(jnp.repeat(d_skip, M2_HEADDIM)), row(m2_norm_w), w_out.astype(F32), row(final_norm_w),
    )
    hbm_operands = (2, 13)
    in_specs = [pl.BlockSpec((ns, tl, d), lambda b, l: (b, l, 0))]
    in_specs += [pl.BlockSpec(memory_space=pl.ANY) if i in hbm_operands else _const_spec(op.shape)
                 for i, op in enumerate(operands) if i > 0]

    kern = functools.partial(_layer_kernel, layer=layer, apply_final_norm=(layer == depth - 1))
    return pl.pallas_call(
        kern,
        grid=(bsz // ns, seq // tl),
        in_specs=in_specs,
        out_specs=pl.BlockSpec((ns, tl, d), lambda b, l: (b, l, 0)),
        out_shape=jax.ShapeDtypeStruct(x.shape, x.dtype),
        scratch_shapes=[
            pltpu.VMEM((ns, HG_HEADS, HG_DV, HG_DK), F32),
            pltpu.VMEM((ns, M2_GROUPS, M2_STATE, M2_GW), F32),
            pltpu.VMEM((ns, CONV_PAD + tl, M2_CONV_DIM), F32),
            pltpu.VMEM((ns, tl, HG_W), BF16),
            pltpu.VMEM((D_MODEL, MAIN_COLS), BF16),
            pltpu.VMEM(w_out.shape, BF16),
            pltpu.VMEM((2, W_IN_STAGE_ROWS, D_MODEL), F32),
            pltpu.VMEM((2, W_OUT_STAGE_ROWS, w_out.shape[1]), F32),
            pltpu.SemaphoreType.DMA((2,)),
            pltpu.SemaphoreType.DMA((2,)),
        ],
        compiler_params=pltpu.CompilerParams(
            dimension_semantics=("arbitrary", "arbitrary"),
            vmem_limit_bytes=VMEM_LIMIT_BYTES,
        ),
        name="hybrid_layer",
    )(*operands)


@jax.jit
def kernel(x, norm_w, w_in, hg_lb_logits, hg_norm_w, m2_conv_w, m2_conv_b, m2_dt_bias, m2_a_log,
           m2_d_skip, m2_norm_w, w_out, final_norm_w):
    depth = w_in.shape[0]
    for l in range(depth):
        x = _layer(x, l, depth, norm_w[l], w_in[l], hg_lb_logits, hg_norm_w[l], m2_conv_w[l],
                   m2_conv_b[l], m2_dt_bias[l], m2_a_log[l], m2_d_skip[l], m2_norm_w[l], w_out[l],
                   final_norm_w)
    return x
```

```python
import functools
import math

import jax
import jax.numpy as jnp
from jax import lax
from jax.experimental import pallas as pl
from jax.experimental.pallas import tpu as pltpu

F32 = jnp.float32
BF16 = jnp.bfloat16

EPS = 1e-6
LOG2E = math.log2(math.e)
D_MODEL = 1024
HG_HEADS = 8
HG_DK = 128
HG_DV = 128
HG_W = HG_HEADS * HG_DK
HG_CHUNK = 128
M2_HEADDIM = 64
M2_W = 1024
M2_HEADS = M2_W // M2_HEADDIM
M2_GROUPS = 2
M2_STATE = 128
M2_CONV = 4
M2_CHUNK = 128
M2_CONV_DIM = M2_W + 2 * M2_GROUPS * M2_STATE
M2_GW = M2_W // M2_GROUPS
M2_HPG = M2_HEADS // M2_GROUPS
M2_QUAD = 4

OFF_Q = 0
OFF_F = OFF_Q + HG_W
OFF_I = OFF_F + HG_W
OFF_G = OFF_I + HG_W
OFF_Z = OFF_G + HG_W
OFF_XBC = OFF_Z + M2_W
OFF_DT = OFF_XBC + M2_CONV_DIM
MAIN_COLS = OFF_DT

SEQ_TILE = 2 * HG_CHUNK
STREAMS = 1
CONV_PAD = 8
W_IN_STAGE_ROWS = 128
W_OUT_STAGE_ROWS = 256
VMEM_LIMIT_BYTES = 58 * 1024 * 1024
HG_SAFE_LOG2_DECAY = -108.0


def _dot(a, b):
    return jnp.dot(a, b, preferred_element_type=F32)


def _dot_nt(a, b):
    return lax.dot_general(a, b, (((1,), (1,)), ((), ())), preferred_element_type=F32)


def _dot_tn(a, b):
    return lax.dot_general(a, b, (((0,), (0,)), ((), ())), preferred_element_type=F32)


def _split3(x):
    hi = x.astype(BF16)
    r1 = x - hi.astype(F32)
    mid = r1.astype(BF16)
    lo = (r1 - mid.astype(F32)).astype(BF16)
    return hi, mid, lo


def _dot_exact_left(mat_bf16, x):
    hi, mid, lo = _split3(x)
    return (_dot(mat_bf16, lo) + _dot(mat_bf16, mid)) + _dot(mat_bf16, hi)


def _dot_exact_right(x, mat_bf16):
    hi, mid, lo = _split3(x)
    return (_dot(lo, mat_bf16) + _dot(mid, mat_bf16)) + _dot(hi, mat_bf16)


def _sigmoid(x):
    return 1.0 / (1.0 + jnp.exp2(x * (-LOG2E)))


def _silu(x):
    return x * _sigmoid(x)


def _softplus(x):
    return jnp.maximum(x, 0.0) + jnp.log(1.0 + jnp.exp(-jnp.abs(x)))


def _iota2(shape):
    return lax.broadcasted_iota(jnp.int32, shape, 0), lax.broadcasted_iota(jnp.int32, shape, 1)


def _block_tril(n, block, upper=False):
    r, c = _iota2((n, n))
    same = (r // block) == (c // block)
    tri = (r <= c) if upper else (c <= r)
    return jnp.where(same & tri, 1.0, 0.0).astype(BF16)


def _block_diag2(a, b):
    z = jnp.zeros_like(a)
    return jnp.concatenate([jnp.concatenate([a, z], axis=1), jnp.concatenate([z, b], axis=1)], axis=0)


def _chunk_cumsum(x, chunk):
    n, w = x.shape
    sub = lax.broadcasted_iota(jnp.int32, (8, w), 0)
    out = []
    run = None
    for g in range(n // 8):
        blk = x[8 * g:8 * g + 8, :]
        for d in (1, 2, 4):
            blk = blk + jnp.where(sub >= d, pltpu.roll(blk, d, axis=0), 0.0)
        if (8 * g) % chunk != 0:
            blk = blk + run
        run = jnp.broadcast_to(blk[7:8, :], (8, w))
        out.append(blk)
    return jnp.concatenate(out, axis=0)


def _rms_rows(o):
    return o * lax.rsqrt(jnp.mean(o * o, axis=-1, keepdims=True) + EPS)


def _hg_same_chunk_exact(q, k, b2, tl):
    r, c = _iota2((tl, tl))
    q_bf = q.astype(BF16)
    k_bf = k.astype(BF16)
    atts = [jnp.where(r == c, _dot_nt(q_bf[:, hd * HG_DK:(hd + 1) * HG_DK], k_bf[:, hd * HG_DK:(hd + 1) * HG_DK]), 0.0)
            for hd in range(HG_HEADS)]
    row = lax.broadcasted_iota(jnp.int32, (tl, 1), 0)
    h = HG_CHUNK // 2
    while h >= 1:
        mid = (r // (2 * h)) * (2 * h) + (h - 1)
        sel = jnp.where(c == mid, 1.0, 0.0).astype(BF16)
        ref = _dot_exact_left(sel, b2)
        e = jnp.exp2(-jnp.abs(b2 - ref))
        is_q = (row % (2 * h)) >= h
        zq = jnp.where(is_q, q * e, 0.0).astype(BF16)
        zk = jnp.where(is_q, 0.0, k * e).astype(BF16)
        pair = ((r // (2 * h)) == (c // (2 * h))) & ((r % (2 * h)) >= h) & ((c % (2 * h)) < h)
        for hd in range(HG_HEADS):
            cols = slice(hd * HG_DK, (hd + 1) * HG_DK)
            atts[hd] = atts[hd] + jnp.where(pair, _dot_nt(zq[:, cols], zk[:, cols]), 0.0)
        h //= 2
    return atts


class _Vals(dict):
    __getattr__ = dict.__getitem__
    __setattr__ = dict.__setitem__


def _front(x, p, st, layer):
    tl = x.shape[0]
    v = _Vals(x=x)
    ms = jnp.mean(x * x, axis=-1, keepdims=True)
    h = (x * lax.rsqrt(ms + EPS)) * p.norm_w[...]
    h_bf = h.astype(BF16)
    h_lo = (h - h_bf.astype(F32)).astype(BF16)

    def proj(off, width):
        return _dot(h_bf, p.w_main[:, off:off + width])

    fl = proj(OFF_F, HG_W)
    q_raw = proj(OFF_Q, HG_W)
    u = proj(OFF_XBC, M2_CONV_DIM)
    wdtT_hi = p.wdtT_hi[...]
    wdtT_lo = p.wdtT_lo[...]
    dt_rawT = (_dot_nt(wdtT_hi, h_lo) + _dot_nt(wdtT_lo, h_bf)) + _dot_nt(wdtT_hi, h_bf)
    v.vT_bf = _dot_nt(p.w_iT[...], h_bf).astype(BF16)
    v.gate_hg = _silu(proj(OFF_G, HG_W)).astype(BF16)
    v.gate_m2 = _silu(proj(OFF_Z, M2_W)).astype(BF16)

    logits = p.lb_logits[...]
    mx = jnp.max(logits, axis=0, keepdims=True)
    ez = jnp.exp(logits - mx)
    lb = jnp.sum(ez[0:layer + 1], axis=0, keepdims=True) / jnp.sum(ez, axis=0, keepdims=True)
    one_m_lb = 1.0 - lb
    sg = _sigmoid(fl)
    f = lb + one_m_lb * sg
    k = one_m_lb * (1.0 - sg)
    b2 = _chunk_cumsum(jnp.log2(f), HG_CHUNK)
    e = jnp.exp2(b2)
    q = _silu(q_raw)
    qt = q * e
    v.q, v.k, v.b2 = q, k, b2
    v.qt_bf = qt.astype(BF16)
    v.kt_bf = (k / e).astype(BF16)
    c0, c1 = _C0, _C1
    tot0 = b2[HG_CHUNK - 1:HG_CHUNK, :]
    tot1 = b2[2 * HG_CHUNK - 1:2 * HG_CHUNK, :]
    khat0 = k[c0, :] * jnp.exp2(tot0 - b2[c0, :])
    khat1 = k[c1, :] * jnp.exp2(tot1 - b2[c1, :])
    v.khat0_bf = khat0.astype(BF16)
    v.kend_bf = jnp.concatenate([(khat0 * jnp.exp2(tot1)).astype(BF16), khat1.astype(BF16)], axis=0)
    v.qg_bf = jnp.concatenate([v.qt_bf[c0, :], (qt[c1, :] * jnp.exp2(tot0)).astype(BF16)], axis=0)
    v.keys1_bf = jnp.concatenate([v.khat0_bf, v.kt_bf[c1, :]], axis=0)
    v.e_tile = jnp.exp2(tot0 + tot1)
    v.same_chunk_safe = jnp.min(jnp.minimum(tot0, tot1)) >= HG_SAFE_LOG2_DECAY

    st.ubuf[CONV_PAD:CONV_PAD + tl, :] = u
    conv = u * p.conv_w[M2_CONV - 1:M2_CONV, :]
    for j in range(M2_CONV - 1):
        sh = M2_CONV - 1 - j
        conv = conv + st.ubuf[CONV_PAD - sh:CONV_PAD - sh + tl, :] * p.conv_w[j:j + 1, :]
    st.ubuf[0:CONV_PAD, :] = st.ubuf[tl:tl + CONV_PAD, :]
    xbc = _silu(conv + p.conv_b[...])
    v.xs = xbc[:, 0:M2_W]
    v.bm_bf = xbc[:, M2_W:M2_W + M2_GROUPS * M2_STATE].astype(BF16)
    v.cm_bf = xbc[:, M2_W + M2_GROUPS * M2_STATE:].astype(BF16)
    v.dtT = _softplus(dt_rawT + p.dt_bias_col[...])
    aT = v.dtT * (jnp.exp(p.a_log_col[...]) * (-LOG2E))
    v.csT = _dot_exact_right(aT, _block_tril(tl, M2_CHUNK, upper=True))
    v.dt = v.dtT.T
    v.cs = v.csT.T
    return v


def _head_cols(hd):
    return slice(hd * HG_DK, (hd + 1) * HG_DK)


def _store_mix_hg(v, p, st, hd, o0, o1):
    cols = _head_cols(hd)
    for rows, o in ((_C0, o0), (_C1, o1)):
        st.mix_hg[rows, cols] = (_rms_rows(o) * p.hg_norm_w[:, cols] * v.gate_hg[rows, cols]).astype(BF16)


_C0 = slice(0, HG_CHUNK)
_C1 = slice(HG_CHUNK, 2 * HG_CHUNK)


def _waves(v, p, st):
    tl = v.x.shape[0]
    c0, c1 = _C0, _C1
    rc, cc = _iota2((HG_CHUNK, HG_CHUNK))
    causal0 = cc <= rc
    r1, k1 = _iota2((HG_CHUNK, tl))
    causal1 = k1 - HG_CHUNK <= r1
    v.o_inter = []
    for pr in range(HG_HEADS // 2):
        st_pair = _block_diag2(st.hg_state[2 * pr].astype(BF16), st.hg_state[2 * pr + 1].astype(BF16))
        oi = _dot_nt(v.qg_bf[:, 2 * pr * HG_DK:(2 * pr + 2) * HG_DK], st_pair)
        v.o_inter += [oi[:, 0:HG_DV], oi[:, HG_DV:2 * HG_DV]]
    att0, att1 = [], []
    for hd in range(HG_HEADS):
        cols = _head_cols(hd)
        att0.append(jnp.where(causal0, _dot_nt(v.qt_bf[c0, cols], v.kt_bf[c0, cols]), 0.0).astype(BF16))
        att1.append(jnp.where(causal1, _dot_nt(v.qt_bf[c1, cols], v.keys1_bf[:, cols]), 0.0).astype(BF16))
        st.hg_state[hd] = st.hg_state[hd] * v.e_tile[:, cols] + _dot(v.vT_bf[cols, :], v.kend_bf[:, cols])

    hr, hc = _iota2((M2_HEADS, M2_W))
    expand = jnp.where(hc // M2_HEADDIM == hr, 1.0, 0.0).astype(BF16)
    r2, c2 = _iota2((M2_CHUNK, M2_CHUNK))
    causal_m2 = c2 <= r2
    qr, qc = _iota2((M2_QUAD * M2_CHUNK, M2_QUAD * M2_HEADDIM))
    quad_mask = (qr // M2_CHUNK) == (qc // M2_HEADDIM)
    nck = tl // M2_CHUNK
    rows_of = lambda c: slice(c * M2_CHUNK, (c + 1) * M2_CHUNK)
    ncols_of = lambda gi: slice(gi * M2_STATE, (gi + 1) * M2_STATE)
    gcols_of = lambda gi: slice(gi * M2_GW, (gi + 1) * M2_GW)
    xs, cs, csT, dtT = v.xs, v.cs, v.csT, v.dtT
    xs_bf = xs.astype(BF16)
    cb = {(c, gi): _dot_nt(v.cm_bf[rows_of(c), ncols_of(gi)], v.bm_bf[rows_of(c), ncols_of(gi)])
          for c in range(nck) for gi in range(M2_GROUPS)}
    ecs_x, xdec_bf = [], []
    for c in range(nck):
        cs_c = cs[rows_of(c), :]
        cs_last = cs_c[M2_CHUNK - 1:M2_CHUNK, :]
        ecs_x.append(_dot(jnp.exp2(cs_c).astype(BF16), expand))
        dec_x = _dot((jnp.exp2(cs_last - cs_c) * v.dt[rows_of(c), :]).astype(BF16), expand)
        xdec_bf.append((xs[rows_of(c), :] * dec_x).astype(BF16))
    y_off = {}
    for gi in range(M2_GROUPS):
        stT = st.m2_state[gi]
        for c in range(nck):
            cg = v.cm_bf[rows_of(c), ncols_of(gi)]
            y_off[c, gi] = _dot(cg, stT.astype(BF16)) * ecs_x[c][:, gcols_of(gi)]
            e_last_x = ecs_x[c][M2_CHUNK - 1:M2_CHUNK, gcols_of(gi)]
            stT = stT * e_last_x + _dot_tn(v.bm_bf[rows_of(c), ncols_of(gi)], xdec_bf[c][:, gcols_of(gi)])
        st.m2_state[gi] = stT

    for hd in range(HG_HEADS):
        cols = _head_cols(hd)
        vT = v.vT_bf[cols, :]
        _store_mix_hg(v, p, st, hd, v.o_inter[hd][c0, :] + _dot_nt(att0[hd], vT[:, c0]),
                      v.o_inter[hd][c1, :] + _dot_nt(att1[hd], vT))

    y_rows = []
    for c in range(nck):
        rows = rows_of(c)
        y_quads = []
        for gi in range(M2_GROUPS):
            for qd in range(M2_HPG // M2_QUAD):
                ws = []
                for hh in range(M2_QUAD):
                    hidx = gi * M2_HPG + qd * M2_QUAD + hh
                    seg = cs[rows, hidx:hidx + 1] - csT[hidx:hidx + 1, rows]
                    lm = jnp.exp2(jnp.where(causal_m2, seg, -jnp.inf))
                    ws.append(((cb[c, gi] * lm) * dtT[hidx:hidx + 1, rows]).astype(BF16))
                q0 = (gi * M2_HPG + qd * M2_QUAD) * M2_HEADDIM
                xq = xs_bf[rows, q0:q0 + M2_QUAD * M2_HEADDIM]
                x_bd = jnp.where(quad_mask, jnp.concatenate([xq] * M2_QUAD, axis=0), jnp.zeros_like(quad_mask, BF16))
                y_diag = _dot(jnp.concatenate(ws, axis=1), x_bd)
                o0 = qd * M2_QUAD * M2_HEADDIM
                y_quads.append(y_diag + y_off[c, gi][:, o0:o0 + M2_QUAD * M2_HEADDIM])
        y_rows.append(jnp.concatenate(y_quads, axis=1))

    y = jnp.concatenate(y_rows, axis=0) + p.d_skip[...] * xs
    y = y * v.gate_m2
    mix_m2 = (jnp.concatenate([_rms_rows(y[:, gi * M2_GW:(gi + 1) * M2_GW]) for gi in range(M2_GROUPS)], axis=-1)
              * p.m2_norm_w[...]).astype(BF16)
    v.out_m2 = _dot(mix_m2, p.w_out[HG_W:HG_W + M2_W, :])


def _redo_heads_exact(v, p, st):
    c0, c1 = _C0, _C1
    same = _hg_same_chunk_exact(v.q, v.k, v.b2, v.x.shape[0])
    for hd in range(HG_HEADS):
        cols = _head_cols(hd)
        vT = v.vT_bf[cols, :]
        cross = _dot_nt(v.qt_bf[c1, cols], v.khat0_bf[:, cols])
        o0 = v.o_inter[hd][c0, :] + _dot_nt(same[hd][c0, c0].astype(BF16), vT[:, c0])
        o1 = (v.o_inter[hd][c1, :] + _dot_nt(cross.astype(BF16), vT[:, c0])
              + _dot_nt(same[hd][c1, c1].astype(BF16), vT[:, c1]))
        _store_mix_hg(v, p, st, hd, o0, o1)


def _tail(v, p, st, apply_final_norm):
    xr = v.x + (_dot(st.mix_hg[...], p.w_out[0:HG_W, :]) + v.out_m2)
    if apply_final_norm:
        ms2 = jnp.mean(xr * xr, axis=-1, keepdims=True)
        xr = (xr * lax.rsqrt(ms2 + EPS)) * p.final_norm_w[...]
    return xr


def _stream_rows(src_hbm, nrows, stage_ref, sem, consume):
    rc = stage_ref.shape[1]
    nchunk = nrows // rc
    assert nchunk * rc == nrows

    def chunk_copy(i):
        return pltpu.make_async_copy(src_hbm.at[pl.ds(i * rc, rc), :], stage_ref.at[i % 2], sem.at[i % 2])

    chunk_copy(0).start()
    for i in range(nchunk):
        if i + 1 < nchunk:
            chunk_copy(i + 1).start()
        chunk_copy(i).wait()
        consume(i, stage_ref[i % 2])


def _load_weights(w_inT_hbm, w_out_hbm, w_main_ref, w_iT_ref, w_out_ref, stage_in_ref, stage_out_ref, sem_in, sem_out):
    rc = stage_in_ref.shape[1]

    def put_in(i, chunk):
        w_main_ref[:, i * rc:(i + 1) * rc] = chunk.T.astype(BF16)
        lo = i * rc - OFF_I
        if 0 <= lo < HG_W:
            w_iT_ref[lo:lo + rc, :] = chunk.astype(BF16)

    def put_out(i, chunk):
        ro = stage_out_ref.shape[1]
        w_out_ref[i * ro:(i + 1) * ro, :] = chunk.astype(BF16)

    _stream_rows(w_inT_hbm, MAIN_COLS, stage_in_ref, sem_in, put_in)
    _stream_rows(w_out_hbm, w_out_hbm.shape[0], stage_out_ref, sem_out, put_out)


def _layer_kernel(x_ref, norm_w_ref, w_inT_hbm, wdtT_hi_ref, wdtT_lo_ref,
                  lb_logits_ref, hg_norm_w_ref, conv_w_ref, conv_b_ref,
                  dt_bias_col_ref, a_log_col_ref,
                  d_skip_ref, m2_norm_w_ref, w_out_hbm, final_norm_w_ref,
                  o_ref,
                  hg_state_ref, m2_state_ref, ubuf_ref, mix_hg_ref,
                  w_main_ref, w_iT_ref, w_out_ref, stage_in_ref, stage_out_ref, sem_in, sem_out,
                  *, layer, apply_final_norm):
    nstream, tl = x_ref.shape[0], x_ref.shape[1]
    assert tl == 2 * HG_CHUNK and tl % M2_CHUNK == 0
    p = _Vals(norm_w=norm_w_ref, w_main=w_main_ref, w_iT=w_iT_ref, wdtT_hi=wdtT_hi_ref, wdtT_lo=wdtT_lo_ref,
              lb_logits=lb_logits_ref, hg_norm_w=hg_norm_w_ref, conv_w=conv_w_ref, conv_b=conv_b_ref,
              dt_bias_col=dt_bias_col_ref, a_log_col=a_log_col_ref, d_skip=d_skip_ref,
              m2_norm_w=m2_norm_w_ref, w_out=w_out_ref, final_norm_w=final_norm_w_ref)
    sts = [_Vals(hg_state=hg_state_ref.at[s], m2_state=m2_state_ref.at[s], ubuf=ubuf_ref.at[s],
                 mix_hg=mix_hg_ref.at[s]) for s in range(nstream)]

    @pl.when((pl.program_id(0) == 0) & (pl.program_id(1) == 0))
    def _():
        _load_weights(w_inT_hbm, w_out_hbm, w_main_ref, w_iT_ref, w_out_ref,
                      stage_in_ref, stage_out_ref, sem_in, sem_out)

    @pl.when(pl.program_id(1) == 0)
    def _():
        hg_state_ref[...] = jnp.zeros_like(hg_state_ref)
        m2_state_ref[...] = jnp.zeros_like(m2_state_ref)
        for s in range(nstream):
            ubuf_ref[s, 0:CONV_PAD, :] = jnp.zeros((CONV_PAD, M2_CONV_DIM), F32)

    vals = []
    for s in range(nstream):
        vals.append(_front(x_ref[s], p, sts[s], layer))
        _waves(vals[s], p, sts[s])

    for s in range(nstream):
        @pl.when(jnp.logical_not(vals[s].same_chunk_safe))
        def _(s=s):
            _redo_heads_exact(vals[s], p, sts[s])

    for s in range(nstream):
        o_ref[s] = _tail(vals[s], p, sts[s], apply_final_norm).astype(o_ref.dtype)


def _const_spec(shape):
    nd = len(shape)
    return pl.BlockSpec(shape, lambda b, l: (0,) * nd, pipeline_mode=pl.Buffered(1))


def _layer(x, layer, depth, norm_w, w_in, hg_lb_logits, hg_norm_w, conv_w, conv_b, dt_bias, a_log,
           d_skip, m2_norm_w, w_out, final_norm_w):
    bsz, seq, d = x.shape
    assert d == D_MODEL and seq % SEQ_TILE == 0 and bsz % STREAMS == 0
    assert w_in.shape == (D_MODEL, MAIN_COLS + M2_HEADS)
    tl, ns = SEQ_TILE, STREAMS

    wdt = w_in[:, MAIN_COLS:]
    wdt_hi = wdt.astype(BF16)
    wdt_lo = (wdt - wdt_hi.astype(F32)).astype(BF16)
    row = lambda t: t.reshape(1, -1).astype(F32)
    col = lambda t: t.reshape(-1, 1).astype(F32)

    operands = (
        x, row(norm_w), w_in.astype(F32).T, wdt_hi.T, wdt_lo.T,
        hg_lb_logits.astype(F32), row(hg_norm_w), conv_w.astype(F32), row(conv_b),
        col(dt_bias), col(a_log),
        row(jnp.repeat(d_skip, M2_HEADDIM)), row(m2_norm_w), w_out.astype(F32), row(final_norm_w),
    )
    hbm_operands = (2, 13)
    in_specs = [pl.BlockSpec((ns, tl, d), lambda b, l: (b, l, 0))]
    in_specs += [pl.BlockSpec(memory_space=pl.ANY) if i in hbm_operands else _const_spec(op.shape)
                 for i, op in enumerate(operands) if i > 0]

    kern = functools.partial(_layer_kernel, layer=layer, apply_final_norm=(layer == depth - 1))
    return pl.pallas_call(
        kern,
        grid=(bsz // ns, seq // tl),
        in_specs=in_specs,
        out_specs=pl.BlockSpec((ns, tl, d), lambda b, l: (b, l, 0)),
        out_shape=jax.ShapeDtypeStruct(x.shape, x.dtype),
        scratch_shapes=[
            pltpu.VMEM((ns, HG_HEADS, HG_DV, HG_DK), F32),
            pltpu.VMEM((ns, M2_GROUPS, M2_STATE, M2_GW), F32),
            pltpu.VMEM((ns, CONV_PAD + tl, M2_CONV_DIM), F32),
            pltpu.VMEM((ns, tl, HG_W), BF16),
            pltpu.VMEM((D_MODEL, MAIN_COLS), BF16),
            pltpu.VMEM((HG_W, D_MODEL), BF16),
            pltpu.VMEM(w_out.shape, BF16),
            pltpu.VMEM((2, W_IN_STAGE_ROWS, D_MODEL), F32),
            pltpu.VMEM((2, W_OUT_STAGE_ROWS, w_out.shape[1]), F32),
            pltpu.SemaphoreType.DMA((2,)),
            pltpu.SemaphoreType.DMA((2,)),
        ],
        compiler_params=pltpu.CompilerParams(
            dimension_semantics=("arbitrary", "arbitrary"),
            vmem_limit_bytes=VMEM_LIMIT_BYTES,
        ),
        name="hybrid_layer",
    )(*operands)


@jax.jit
def kernel(x, norm_w, w_in, hg_lb_logits, hg_norm_w, m2_conv_w, m2_conv_b, m2_dt_bias, m2_a_log,
           m2_d_skip, m2_norm_w, w_out, final_norm_w):
    depth = w_in.shape[0]
    for l in range(depth):
        x = _layer(x, l, depth, norm_w[l], w_in[l], hg_lb_logits, hg_norm_w[l], m2_conv_w[l],
                   m2_conv_b[l], m2_dt_bias[l], m2_a_log[l], m2_d_skip[l], m2_norm_w[l], w_out[l],
                   final_norm_w)
    return x
```

```python
import functools
import math

import jax
import jax.numpy as jnp
from jax import lax
from jax.experimental import pallas as pl
from jax.experimental.pallas import tpu as pltpu

F32 = jnp.float32
BF16 = jnp.bfloat16

EPS = 1e-6
LOG2E = math.log2(math.e)
D_MODEL = 1024
HG_HEADS = 8
HG_DK = 128
HG_DV = 128
HG_W = HG_HEADS * HG_DK
HG_CHUNK = 128
M2_HEADDIM = 64
M2_W = 1024
M2_HEADS = M2_W // M2_HEADDIM
M2_GROUPS = 2
M2_STATE = 128
M2_CONV = 4
M2_CHUNK = 128
M2_CONV_DIM = M2_W + 2 * M2_GROUPS * M2_STATE
M2_GW = M2_W // M2_GROUPS
M2_HPG = M2_HEADS // M2_GROUPS
M2_QUAD = 4

OFF_Q = 0
OFF_F = OFF_Q + HG_W
OFF_I = OFF_F + HG_W
OFF_G = OFF_I + HG_W
OFF_Z = OFF_G + HG_W
OFF_XBC = OFF_Z + M2_W
OFF_DT = OFF_XBC + M2_CONV_DIM
MAIN_COLS = OFF_DT

SEQ_TILE = 2 * HG_CHUNK
STREAMS = 1
CONV_PAD = 8
W_IN_STAGE_ROWS = 128
W_OUT_STAGE_ROWS = 256
VMEM_LIMIT_BYTES = 58 * 1024 * 1024
HG_SAFE_LOG2_DECAY = -108.0


def _dot(a, b):
    return jnp.dot(a, b, preferred_element_type=F32)


def _dot_nt(a, b):
    return lax.dot_general(a, b, (((1,), (1,)), ((), ())), preferred_element_type=F32)


def _dot_tn(a, b):
    return lax.dot_general(a, b, (((0,), (0,)), ((), ())), preferred_element_type=F32)


def _split3(x):
    hi = x.astype(BF16)
    r1 = x - hi.astype(F32)
    mid = r1.astype(BF16)
    lo = (r1 - mid.astype(F32)).astype(BF16)
    return hi, mid, lo


def _dot_exact_left(mat_bf16, x):
    hi, mid, lo = _split3(x)
    return (_dot(mat_bf16, lo) + _dot(mat_bf16, mid)) + _dot(mat_bf16, hi)


def _dot_exact_right(x, mat_bf16):
    hi, mid, lo = _split3(x)
    return (_dot(lo, mat_bf16) + _dot(mid, mat_bf16)) + _dot(hi, mat_bf16)


def _sigmoid(x):
    return 1.0 / (1.0 + jnp.exp2(x * (-LOG2E)))


def _silu(x):
    return x * _sigmoid(x)


def _softplus(x):
    return jnp.maximum(x, 0.0) + jnp.log(1.0 + jnp.exp(-jnp.abs(x)))


def _iota2(shape):
    return lax.broadcasted_iota(jnp.int32, shape, 0), lax.broadcasted_iota(jnp.int32, shape, 1)


def _block_tril(n, block, upper=False):
    r, c = _iota2((n, n))
    same = (r // block) == (c // block)
    tri = (r <= c) if upper else (c <= r)
    return jnp.where(same & tri, 1.0, 0.0).astype(BF16)


def _block_diag2(a, b):
    z = jnp.zeros_like(a)
    return jnp.concatenate([jnp.concatenate([a, z], axis=1), jnp.concatenate([z, b], axis=1)], axis=0)


def _chunk_cumsum(x, chunk):
    n, w = x.shape
    sub = lax.broadcasted_iota(jnp.int32, (8, w), 0)
    out = []
    run = None
    for g in range(n // 8):
        blk = x[8 * g:8 * g + 8, :]
        for d in (1, 2, 4):
            blk = blk + jnp.where(sub >= d, pltpu.roll(blk, d, axis=0), 0.0)
        if (8 * g) % chunk != 0:
            blk = blk + run
        run = jnp.broadcast_to(blk[7:8, :], (8, w))
        out.append(blk)
    return jnp.concatenate(out, axis=0)


def _rms_rows(o):
    return o * lax.rsqrt(jnp.mean(o * o, axis=-1, keepdims=True) + EPS)


def _hg_same_chunk_exact(q, k, b2, tl):
    r, c = _iota2((tl, tl))
    q_bf = q.astype(BF16)
    k_bf = k.astype(BF16)
    atts = [jnp.where(r == c, _dot_nt(q_bf[:, hd * HG_DK:(hd + 1) * HG_DK], k_bf[:, hd * HG_DK:(hd + 1) * HG_DK]), 0.0)
            for hd in range(HG_HEADS)]
    row = lax.broadcasted_iota(jnp.int32, (tl, 1), 0)
    h = HG_CHUNK // 2
    while h >= 1:
        mid = (r // (2 * h)) * (2 * h) + (h - 1)
        sel = jnp.where(c == mid, 1.0, 0.0).astype(BF16)
        ref = _dot_exact_left(sel, b2)
        e = jnp.exp2(-jnp.abs(b2 - ref))
        is_q = (row % (2 * h)) >= h
        zq = jnp.where(is_q, q * e, 0.0).astype(BF16)
        zk = jnp.where(is_q, 0.0, k * e).astype(BF16)
        pair = ((r // (2 * h)) == (c // (2 * h))) & ((r % (2 * h)) >= h) & ((c % (2 * h)) < h)
        for hd in range(HG_HEADS):
            cols = slice(hd * HG_DK, (hd + 1) * HG_DK)
            atts[hd] = atts[hd] + jnp.where(pair, _dot_nt(zq[:, cols], zk[:, cols]), 0.0)
        h //= 2
    return atts


class _Vals(dict):
    __getattr__ = dict.__getitem__
    __setattr__ = dict.__setitem__


def _front(x, p, st, layer):
    tl = x.shape[0]
    v = _Vals(x=x)
    ms = jnp.mean(x * x, axis=-1, keepdims=True)
    h = (x * lax.rsqrt(ms + EPS)) * p.norm_w[...]
    h_bf = h.astype(BF16)
    h_lo = (h - h_bf.astype(F32)).astype(BF16)

    def proj(off, width):
        return _dot(h_bf, p.w_main[:, off:off + width])

    fl = proj(OFF_F, HG_W)
    wdtT_hi = p.wdtT_hi[...]
    wdtT_lo = p.wdtT_lo[...]
    dt_rawT = (_dot_nt(wdtT_hi, h_lo) + _dot_nt(wdtT_lo, h_bf)) + _dot_nt(wdtT_hi, h_bf)
    q_raw = proj(OFF_Q, HG_W)
    v.dtT = _softplus(dt_rawT + p.dt_bias_col[...])
    aT = v.dtT * (jnp.exp(p.a_log_col[...]) * (-LOG2E))
    u = proj(OFF_XBC, M2_CONV_DIM)
    v.csT = _dot_exact_right(aT, _block_tril(tl, M2_CHUNK, upper=True))
    v.vT_bf = _dot_nt(p.w_iT[...], h_bf).astype(BF16)
    v.g_raw = proj(OFF_G, HG_W)
    v.z_raw = proj(OFF_Z, M2_W)

    logits = p.lb_logits[...]
    mx = jnp.max(logits, axis=0, keepdims=True)
    ez = jnp.exp(logits - mx)
    lb = jnp.sum(ez[0:layer + 1], axis=0, keepdims=True) / jnp.sum(ez, axis=0, keepdims=True)
    one_m_lb = 1.0 - lb
    sg = _sigmoid(fl)
    f = lb + one_m_lb * sg
    k = one_m_lb * (1.0 - sg)
    b2 = _chunk_cumsum(jnp.log2(f), HG_CHUNK)
    e = jnp.exp2(b2)
    q = _silu(q_raw)
    qt = q * e
    v.q, v.k, v.b2 = q, k, b2
    v.qt_bf = qt.astype(BF16)
    v.kt_bf = (k / e).astype(BF16)
    c0, c1 = _C0, _C1
    tot0 = b2[HG_CHUNK - 1:HG_CHUNK, :]
    tot1 = b2[2 * HG_CHUNK - 1:2 * HG_CHUNK, :]
    khat0 = k[c0, :] * jnp.exp2(tot0 - b2[c0, :])
    khat1 = k[c1, :] * jnp.exp2(tot1 - b2[c1, :])
    v.khat0_bf = khat0.astype(BF16)
    v.kend_bf = jnp.concatenate([(khat0 * jnp.exp2(tot1)).astype(BF16), khat1.astype(BF16)], axis=0)
    v.qg_bf = jnp.concatenate([v.qt_bf[c0, :], (qt[c1, :] * jnp.exp2(tot0)).astype(BF16)], axis=0)
    v.keys1_bf = jnp.concatenate([v.khat0_bf, v.kt_bf[c1, :]], axis=0)
    v.e_tile = jnp.exp2(tot0 + tot1)
    v.same_chunk_safe = jnp.min(jnp.minimum(tot0, tot1)) >= HG_SAFE_LOG2_DECAY

    st.ubuf[CONV_PAD:CONV_PAD + tl, :] = u
    conv = u * p.conv_w[M2_CONV - 1:M2_CONV, :]
    for j in range(M2_CONV - 1):
        sh = M2_CONV - 1 - j
        conv = conv + st.ubuf[CONV_PAD - sh:CONV_PAD - sh + tl, :] * p.conv_w[j:j + 1, :]
    st.ubuf[0:CONV_PAD, :] = st.ubuf[tl:tl + CONV_PAD, :]
    xbc = _silu(conv + p.conv_b[...])
    v.xs = xbc[:, 0:M2_W]
    v.bm_bf = xbc[:, M2_W:M2_W + M2_GROUPS * M2_STATE].astype(BF16)
    v.cm_bf = xbc[:, M2_W + M2_GROUPS * M2_STATE:].astype(BF16)
    v.dt = v.dtT.T
    v.cs = v.csT.T
    return v


def _head_cols(hd):
    return slice(hd * HG_DK, (hd + 1) * HG_DK)


def _store_mix_hg(v, p, st, hd, o0, o1):
    cols = _head_cols(hd)
    for rows, o in ((_C0, o0), (_C1, o1)):
        st.o_hg[rows, cols] = _rms_rows(o)


_C0 = slice(0, HG_CHUNK)
_C1 = slice(HG_CHUNK, 2 * HG_CHUNK)


def _waves(v, p, st):
    tl = v.x.shape[0]
    c0, c1 = _C0, _C1
    rc, cc = _iota2((HG_CHUNK, HG_CHUNK))
    causal0 = cc <= rc
    r1, k1 = _iota2((HG_CHUNK, tl))
    causal1 = k1 - HG_CHUNK <= r1
    v.o_inter = []
    for pr in range(HG_HEADS // 2):
        st_pair = _block_diag2(st.hg_state[2 * pr].astype(BF16), st.hg_state[2 * pr + 1].astype(BF16))
        oi = _dot_nt(v.qg_bf[:, 2 * pr * HG_DK:(2 * pr + 2) * HG_DK], st_pair)
        v.o_inter += [oi[:, 0:HG_DV], oi[:, HG_DV:2 * HG_DV]]
    att0, att1 = [], []
    for hd in range(HG_HEADS):
        cols = _head_cols(hd)
        att0.append(jnp.where(causal0, _dot_nt(v.qt_bf[c0, cols], v.kt_bf[c0, cols]), 0.0).astype(BF16))
        att1.append(jnp.where(causal1, _dot_nt(v.qt_bf[c1, cols], v.keys1_bf[:, cols]), 0.0).astype(BF16))
    for hd in range(HG_HEADS):
        cols = _head_cols(hd)
        st.hg_state[hd] = st.hg_state[hd] * v.e_tile[:, cols] + _dot(v.vT_bf[cols, :], v.kend_bf[:, cols])

    hr, hc = _iota2((M2_HEADS, M2_W))
    expand = jnp.where(hc // M2_HEADDIM == hr, 1.0, 0.0).astype(BF16)
    r2, c2 = _iota2((M2_CHUNK, M2_CHUNK))
    causal_m2 = c2 <= r2
    qr, qc = _iota2((M2_QUAD * M2_CHUNK, M2_QUAD * M2_HEADDIM))
    quad_mask = (qr // M2_CHUNK) == (qc // M2_HEADDIM)
    nck = tl // M2_CHUNK
    rows_of = lambda c: slice(c * M2_CHUNK, (c + 1) * M2_CHUNK)
    ncols_of = lambda gi: slice(gi * M2_STATE, (gi + 1) * M2_STATE)
    gcols_of = lambda gi: slice(gi * M2_GW, (gi + 1) * M2_GW)
    xs, cs, csT, dtT = v.xs, v.cs, v.csT, v.dtT
    xs_bf = xs.astype(BF16)
    cb = {(c, gi): _dot_nt(v.cm_bf[rows_of(c), ncols_of(gi)], v.bm_bf[rows_of(c), ncols_of(gi)])
          for c in range(nck) for gi in range(M2_GROUPS)}
    ecs_x, xdec_bf = [], []
    for c in range(nck):
        cs_c = cs[rows_of(c), :]
        cs_last = cs_c[M2_CHUNK - 1:M2_CHUNK, :]
        ecs_x.append(_dot(jnp.exp2(cs_c).astype(BF16), expand))
        dec_x = _dot((jnp.exp2(cs_last - cs_c) * v.dt[rows_of(c), :]).astype(BF16), expand)
        xdec_bf.append((xs[rows_of(c), :] * dec_x).astype(BF16))
    y_off = {}
    for gi in range(M2_GROUPS):
        stT = st.m2_state[gi]
        for c in range(nck):
            cg = v.cm_bf[rows_of(c), ncols_of(gi)]
            y_off[c, gi] = _dot(cg, stT.astype(BF16)) * ecs_x[c][:, gcols_of(gi)]
            e_last_x = ecs_x[c][M2_CHUNK - 1:M2_CHUNK, gcols_of(gi)]
            stT = stT * e_last_x + _dot_tn(v.bm_bf[rows_of(c), ncols_of(gi)], xdec_bf[c][:, gcols_of(gi)])
        st.m2_state[gi] = stT

    for hd in range(HG_HEADS):
        cols = _head_cols(hd)
        vT = v.vT_bf[cols, :]
        _store_mix_hg(v, p, st, hd, v.o_inter[hd][c0, :] + _dot_nt(att0[hd], vT[:, c0]),
                      v.o_inter[hd][c1, :] + _dot_nt(att1[hd], vT))

    for c in range(nck):
        rows = rows_of(c)
        for gi in range(M2_GROUPS):
            for qd in range(M2_HPG // M2_QUAD):
                ws = []
                for hh in range(M2_QUAD):
                    hidx = gi * M2_HPG + qd * M2_QUAD + hh
                    seg = cs[rows, hidx:hidx + 1] - csT[hidx:hidx + 1, rows]
                    lm = jnp.exp2(jnp.where(causal_m2, seg, -jnp.inf))
                    ws.append(((cb[c, gi] * lm) * dtT[hidx:hidx + 1, rows]).astype(BF16))
                q0 = (gi * M2_HPG + qd * M2_QUAD) * M2_HEADDIM
                xq = xs_bf[rows, q0:q0 + M2_QUAD * M2_HEADDIM]
                x_bd = jnp.where(quad_mask, jnp.concatenate([xq] * M2_QUAD, axis=0), jnp.zeros_like(quad_mask, BF16))
                y_diag = _dot(jnp.concatenate(ws, axis=1), x_bd)
                o0 = qd * M2_QUAD * M2_HEADDIM
                st.y_m2[rows, q0:q0 + M2_QUAD * M2_HEADDIM] = y_diag + y_off[c, gi][:, o0:o0 + M2_QUAD * M2_HEADDIM]

    y = st.y_m2[...] + p.d_skip[...] * xs
    y = y * _silu(v.z_raw)
    mix_m2 = (jnp.concatenate([_rms_rows(y[:, gi * M2_GW:(gi + 1) * M2_GW]) for gi in range(M2_GROUPS)], axis=-1)
              * p.m2_norm_w[...]).astype(BF16)
    v.out_m2 = _dot(mix_m2, p.w_out[HG_W:HG_W + M2_W, :])


def _redo_heads_exact(v, p, st):
    c0, c1 = _C0, _C1
    same = _hg_same_chunk_exact(v.q, v.k, v.b2, v.x.shape[0])
    for hd in range(HG_HEADS):
        cols = _head_cols(hd)
        vT = v.vT_bf[cols, :]
        cross = _dot_nt(v.qt_bf[c1, cols], v.khat0_bf[:, cols])
        o0 = v.o_inter[hd][c0, :] + _dot_nt(same[hd][c0, c0].astype(BF16), vT[:, c0])
        o1 = (v.o_inter[hd][c1, :] + _dot_nt(cross.astype(BF16), vT[:, c0])
              + _dot_nt(same[hd][c1, c1].astype(BF16), vT[:, c1]))
        _store_mix_hg(v, p, st, hd, o0, o1)


def _tail(v, p, st, apply_final_norm):
    mix_hg = (st.o_hg[...] * p.hg_norm_w[...] * _silu(v.g_raw)).astype(BF16)
    xr = v.x + (_dot(mix_hg, p.w_out[0:HG_W, :]) + v.out_m2)
    if apply_final_norm:
        ms2 = jnp.mean(xr * xr, axis=-1, keepdims=True)
        xr = (xr * lax.rsqrt(ms2 + EPS)) * p.final_norm_w[...]
    return xr


def _stream_rows(src_hbm, nrows, stage_ref, sem, consume):
    rc = stage_ref.shape[1]
    nchunk = nrows // rc
    assert nchunk * rc == nrows

    def chunk_copy(i):
        return pltpu.make_async_copy(src_hbm.at[pl.ds(i * rc, rc), :], stage_ref.at[i % 2], sem.at[i % 2])

    chunk_copy(0).start()
    for i in range(nchunk):
        if i + 1 < nchunk:
            chunk_copy(i + 1).start()
        chunk_copy(i).wait()
        consume(i, stage_ref[i % 2])


def _load_weights(w_inT_hbm, w_out_hbm, w_main_ref, w_iT_ref, w_out_ref, stage_in_ref, stage_out_ref, sem_in, sem_out):
    rc = stage_in_ref.shape[1]

    def put_in(i, chunk):
        w_main_ref[:, i * rc:(i + 1) * rc] = chunk.T.astype(BF16)
        lo = i * rc - OFF_I
        if 0 <= lo < HG_W:
            w_iT_ref[lo:lo + rc, :] = chunk.astype(BF16)

    def put_out(i, chunk):
        ro = stage_out_ref.shape[1]
        w_out_ref[i * ro:(i + 1) * ro, :] = chunk.astype(BF16)

    _stream_rows(w_inT_hbm, MAIN_COLS, stage_in_ref, sem_in, put_in)
    _stream_rows(w_out_hbm, w_out_hbm.shape[0], stage_out_ref, sem_out, put_out)


def _layer_kernel(x_ref, norm_w_ref, w_inT_hbm, wdtT_hi_ref, wdtT_lo_ref,
                  lb_logits_ref, hg_norm_w_ref, conv_w_ref, conv_b_ref,
                  dt_bias_col_ref, a_log_col_ref,
                  d_skip_ref, m2_norm_w_ref, w_out_hbm, final_norm_w_ref,
                  o_ref,
                  hg_state_ref, m2_state_ref, ubuf_ref, o_hg_ref, y_m2_ref,
                  w_main_ref, w_iT_ref, w_out_ref, stage_in_ref, stage_out_ref, sem_in, sem_out,
                  *, layer, apply_final_norm):
    nstream, tl = x_ref.shape[0], x_ref.shape[1]
    assert tl == 2 * HG_CHUNK and tl % M2_CHUNK == 0
    p = _Vals(norm_w=norm_w_ref, w_main=w_main_ref, w_iT=w_iT_ref, wdtT_hi=wdtT_hi_ref, wdtT_lo=wdtT_lo_ref,
              lb_logits=lb_logits_ref, hg_norm_w=hg_norm_w_ref, conv_w=conv_w_ref, conv_b=conv_b_ref,
              dt_bias_col=dt_bias_col_ref, a_log_col=a_log_col_ref, d_skip=d_skip_ref,
              m2_norm_w=m2_norm_w_ref, w_out=w_out_ref, final_norm_w=final_norm_w_ref)
    sts = [_Vals(hg_state=hg_state_ref.at[s], m2_state=m2_state_ref.at[s], ubuf=ubuf_ref.at[s],
                 o_hg=o_hg_ref.at[s], y_m2=y_m2_ref.at[s]) for s in range(nstream)]

    @pl.when((pl.program_id(0) == 0) & (pl.program_id(1) == 0))
    def _():
        _load_weights(w_inT_hbm, w_out_hbm, w_main_ref, w_iT_ref, w_out_ref,
                      stage_in_ref, stage_out_ref, sem_in, sem_out)

    @pl.when(pl.program_id(1) == 0)
    def _():
        hg_state_ref[...] = jnp.zeros_like(hg_state_ref)
        m2_state_ref[...] = jnp.zeros_like(m2_state_ref)
        for s in range(nstream):
            ubuf_ref[s, 0:CONV_PAD, :] = jnp.zeros((CONV_PAD, M2_CONV_DIM), F32)

    vals = []
    for s in range(nstream):
        vals.append(_front(x_ref[s], p, sts[s], layer))
        _waves(vals[s], p, sts[s])

    for s in range(nstream):
        @pl.when(jnp.logical_not(vals[s].same_chunk_safe))
        def _(s=s):
            _redo_heads_exact(vals[s], p, sts[s])

    for s in range(nstream):
        o_ref[s] = _tail(vals[s], p, sts[s], apply_final_norm).astype(o_ref.dtype)


def _const_spec(shape):
    nd = len(shape)
    return pl.BlockSpec(shape, lambda b, l: (0,) * nd, pipeline_mode=pl.Buffered(1))


def _layer(x, layer, depth, norm_w, w_in, hg_lb_logits, hg_norm_w, conv_w, conv_b, dt_bias, a_log,
           d_skip, m2_norm_w, w_out, final_norm_w):
    bsz, seq, d = x.shape
    assert d == D_MODEL and seq % SEQ_TILE == 0 and bsz % STREAMS == 0
    assert w_in.shape == (D_MODEL, MAIN_COLS + M2_HEADS)
    tl, ns = SEQ_TILE, STREAMS

    wdt = w_in[:, MAIN_COLS:]
    wdt_hi = wdt.astype(BF16)
    wdt_lo = (wdt - wdt_hi.astype(F32)).astype(BF16)
    row = lambda t: t.reshape(1, -1).astype(F32)
    col = lambda t: t.reshape(-1, 1).astype(F32)

    operands = (
        x, row(norm_w), w_in.astype(F32).T, wdt_hi.T, wdt_lo.T,
        hg_lb_logits.astype(F32), row(hg_norm_w), conv_w.astype(F32), row(conv_b),
        col(dt_bias), col(a_log),
        row(jnp.repeat(d_skip, M2_HEADDIM)), row(m2_norm_w), w_out.astype(F32), row(final_norm_w),
    )
    hbm_operands = (2, 13)
    in_specs = [pl.BlockSpec((ns, tl, d), lambda b, l: (b, l, 0))]
    in_specs += [pl.BlockSpec(memory_space=pl.ANY) if i in hbm_operands else _const_spec(op.shape)
                 for i, op in enumerate(operands) if i > 0]

    kern = functools.partial(_layer_kernel, layer=layer, apply_final_norm=(layer == depth - 1))
    return pl.pallas_call(
        kern,
        grid=(bsz // ns, seq // tl),
        in_specs=in_specs,
        out_specs=pl.BlockSpec((ns, tl, d), lambda b, l: (b, l, 0)),
        out_shape=jax.ShapeDtypeStruct(x.shape, x.dtype),
        scratch_shapes=[
            pltpu.VMEM((ns, HG_HEADS, HG_DV, HG_DK), F32),
            pltpu.VMEM((ns, M2_GROUPS, M2_STATE, M2_GW), F32),
            pltpu.VMEM((ns, CONV_PAD + tl, M2_CONV_DIM), F32),
            pltpu.VMEM((ns, tl, HG_W), F32),
            pltpu.VMEM((ns, tl, M2_W), F32),
            pltpu.VMEM((D_MODEL, MAIN_COLS), BF16),
            pltpu.VMEM((HG_W, D_MODEL), BF16),
            pltpu.VMEM(w_out.shape, BF16),
            pltpu.VMEM((2, W_IN_STAGE_ROWS, D_MODEL), F32),
            pltpu.VMEM((2, W_OUT_STAGE_ROWS, w_out.shape[1]), F32),
            pltpu.SemaphoreType.DMA((2,)),
            pltpu.SemaphoreType.DMA((2,)),
        ],
        compiler_params=pltpu.CompilerParams(
            dimension_semantics=("arbitrary", "arbitrary"),
            vmem_limit_bytes=VMEM_LIMIT_BYTES,
        ),
        name="hybrid_layer",
    )(*operands)


@jax.jit
def kernel(x, norm_w, w_in, hg_lb_logits, hg_norm_w, m2_conv_w, m2_conv_b, m2_dt_bias, m2_a_log,
           m2_d_skip, m2_norm_w, w_out, final_norm_w):
    depth = w_in.shape[0]
    for l in range(depth):
        x = _layer(x, l, depth, norm_w[l], w_in[l], hg_lb_logits, hg_norm_w[l], m2_conv_w[l],
                   m2_conv_b[l], m2_dt_bias[l], m2_a_log[l], m2_d_skip[l], m2_norm_w[l], w_out[l],
                   final_norm_w)
    return x
```

```python
import functools
import math

import jax
import jax.numpy as jnp
from jax import lax
from jax.experimental import pallas as pl
from jax.experimental.pallas import tpu as pltpu

F32 = jnp.float32
BF16 = jnp.bfloat16

EPS = 1e-6
LOG2E = math.log2(math.e)
D_MODEL = 1024
HG_HEADS = 8
HG_DK = 128
HG_DV = 128
HG_W = HG_HEADS * HG_DK
HG_CHUNK = 128
M2_HEADDIM = 64
M2_W = 1024
M2_HEADS = M2_W // M2_HEADDIM
M2_GROUPS = 2
M2_STATE = 128
M2_CONV = 4
M2_CHUNK = 128
M2_CONV_DIM = M2_W + 2 * M2_GROUPS * M2_STATE
M2_GW = M2_W // M2_GROUPS
M2_HPG = M2_HEADS // M2_GROUPS
M2_QUAD = 4

OFF_Q = 0
OFF_F = OFF_Q + HG_W
OFF_I = OFF_F + HG_W
OFF_G = OFF_I + HG_W
OFF_Z = OFF_G + HG_W
OFF_XBC = OFF_Z + M2_W
OFF_DT = OFF_XBC + M2_CONV_DIM
MAIN_COLS = OFF_DT

SEQ_TILE = 2 * HG_CHUNK
CONV_PAD = 8
W_IN_STAGE_ROWS = 128
W_OUT_STAGE_ROWS = 256
VMEM_LIMIT_BYTES = 58 * 1024 * 1024
HG_SAFE_LOG2_DECAY = -108.0


def _dot(a, b):
    return jnp.dot(a, b, preferred_element_type=F32)


def _dot_nt(a, b):
    return lax.dot_general(a, b, (((1,), (1,)), ((), ())), preferred_element_type=F32)


def _dot_tn(a, b):
    return lax.dot_general(a, b, (((0,), (0,)), ((), ())), preferred_element_type=F32)


def _split3(x):
    hi = x.astype(BF16)
    r1 = x - hi.astype(F32)
    mid = r1.astype(BF16)
    lo = (r1 - mid.astype(F32)).astype(BF16)
    return hi, mid, lo


def _dot_exact_left(mat_bf16, x):
    hi, mid, lo = _split3(x)
    return (_dot(mat_bf16, lo) + _dot(mat_bf16, mid)) + _dot(mat_bf16, hi)


def _dot_exact_right(x, mat_bf16):
    hi, mid, lo = _split3(x)
    return (_dot(lo, mat_bf16) + _dot(mid, mat_bf16)) + _dot(hi, mat_bf16)


def _sigmoid(x):
    return 1.0 / (1.0 + jnp.exp2(x * (-LOG2E)))


def _silu(x):
    return x * _sigmoid(x)


def _softplus(x):
    return jnp.maximum(x, 0.0) + jnp.log(1.0 + jnp.exp(-jnp.abs(x)))


def _iota2(shape):
    return lax.broadcasted_iota(jnp.int32, shape, 0), lax.broadcasted_iota(jnp.int32, shape, 1)


def _block_tril(n, block, upper=False):
    r, c = _iota2((n, n))
    same = (r // block) == (c // block)
    tri = (r <= c) if upper else (c <= r)
    return jnp.where(same & tri, 1.0, 0.0).astype(BF16)


def _block_diag2(a, b):
    z = jnp.zeros_like(a)
    return jnp.concatenate([jnp.concatenate([a, z], axis=1), jnp.concatenate([z, b], axis=1)], axis=0)


def _chunk_cumsum(x, chunk):
    n, w = x.shape
    sub = lax.broadcasted_iota(jnp.int32, (8, w), 0)
    out = []
    run = None
    for g in range(n // 8):
        blk = x[8 * g:8 * g + 8, :]
        for d in (1, 2, 4):
            blk = blk + jnp.where(sub >= d, pltpu.roll(blk, d, axis=0), 0.0)
        if (8 * g) % chunk != 0:
            blk = blk + run
        run = jnp.broadcast_to(blk[7:8, :], (8, w))
        out.append(blk)
    return jnp.concatenate(out, axis=0)


def _rms_rows(o):
    return o * lax.rsqrt(jnp.mean(o * o, axis=-1, keepdims=True) + EPS)


def _hg_same_chunk_exact(q, k, b2, tl):
    r, c = _iota2((tl, tl))
    q_bf = q.astype(BF16)
    k_bf = k.astype(BF16)
    atts = [jnp.where(r == c, _dot_nt(q_bf[:, hd * HG_DK:(hd + 1) * HG_DK], k_bf[:, hd * HG_DK:(hd + 1) * HG_DK]), 0.0)
            for hd in range(HG_HEADS)]
    row = lax.broadcasted_iota(jnp.int32, (tl, 1), 0)
    h = HG_CHUNK // 2
    while h >= 1:
        mid = (r // (2 * h)) * (2 * h) + (h - 1)
        sel = jnp.where(c == mid, 1.0, 0.0).astype(BF16)
        ref = _dot_exact_left(sel, b2)
        e = jnp.exp2(-jnp.abs(b2 - ref))
        is_q = (row % (2 * h)) >= h
        zq = jnp.where(is_q, q * e, 0.0).astype(BF16)
        zk = jnp.where(is_q, 0.0, k * e).astype(BF16)
        pair = ((r // (2 * h)) == (c // (2 * h))) & ((r % (2 * h)) >= h) & ((c % (2 * h)) < h)
        for hd in range(HG_HEADS):
            cols = slice(hd * HG_DK, (hd + 1) * HG_DK)
            atts[hd] = atts[hd] + jnp.where(pair, _dot_nt(zq[:, cols], zk[:, cols]), 0.0)
        h //= 2
    return atts


def _stream_rows(src_hbm, nrows, stage_ref, sem, consume):
    rc = stage_ref.shape[1]
    nchunk = nrows // rc
    assert nchunk * rc == nrows

    def chunk_copy(i):
        return pltpu.make_async_copy(src_hbm.at[pl.ds(i * rc, rc), :], stage_ref.at[i % 2], sem.at[i % 2])

    chunk_copy(0).start()
    for i in range(nchunk):
        if i + 1 < nchunk:
            chunk_copy(i + 1).start()
        chunk_copy(i).wait()
        consume(i, stage_ref[i % 2])


def _load_weights(w_inT_hbm, w_out_hbm, w_main_ref, w_out_ref, stage_in_ref, stage_out_ref, sem_in, sem_out):
    rc = stage_in_ref.shape[1]
    ro = stage_out_ref.shape[1]

    def put_in(i, chunk):
        w_main_ref[:, i * rc:(i + 1) * rc] = chunk.T.astype(BF16)

    def put_out(i, chunk):
        w_out_ref[i * ro:(i + 1) * ro, :] = chunk.astype(BF16)

    _stream_rows(w_inT_hbm, MAIN_COLS, stage_in_ref, sem_in, put_in)
    _stream_rows(w_out_hbm, w_out_hbm.shape[0], stage_out_ref, sem_out, put_out)


def _layer_kernel(x_ref, norm_w_ref, w_inT_hbm, wdtT_hi_ref, wdtT_lo_ref,
                  lb_logits_ref, hg_norm_w_ref, conv_w_ref, conv_b_ref,
                  dt_bias_col_ref, a_log_col_ref,
                  d_skip_ref, m2_norm_w_ref, w_out_hbm, final_norm_w_ref,
                  o_ref,
                  hg_state_ref, m2_state_ref, ubuf_ref, o_hg_ref, y_m2_ref,
                  w_main_ref, w_out_ref, stage_in_ref, stage_out_ref, sem_in, sem_out,
                  *, layer, apply_final_norm):
    tl = x_ref.shape[1]
    assert tl == 2 * HG_CHUNK and tl % M2_CHUNK == 0
    lt = pl.program_id(1)

    @pl.when((pl.program_id(0) == 0) & (lt == 0))
    def _():
        _load_weights(w_inT_hbm, w_out_hbm, w_main_ref, w_out_ref, stage_in_ref, stage_out_ref, sem_in, sem_out)

    @pl.when(lt == 0)
    def _():
        hg_state_ref[...] = jnp.zeros_like(hg_state_ref)
        m2_state_ref[...] = jnp.zeros_like(m2_state_ref)
        ubuf_ref[0:CONV_PAD, :] = jnp.zeros((CONV_PAD, M2_CONV_DIM), F32)

    x = x_ref[0]
    ms = jnp.mean(x * x, axis=-1, keepdims=True)
    h = (x * lax.rsqrt(ms + EPS)) * norm_w_ref[...]
    h_bf = h.astype(BF16)
    h_lo = (h - h_bf.astype(F32)).astype(BF16)

    def proj(off, width):
        return _dot(h_bf, w_main_ref[:, off:off + width])

    fl = proj(OFF_F, HG_W)
    q_raw = proj(OFF_Q, HG_W)
    u = proj(OFF_XBC, M2_CONV_DIM)
    wdtT_hi = wdtT_hi_ref[...]
    wdtT_lo = wdtT_lo_ref[...]
    dt_rawT = (_dot_nt(wdtT_hi, h_lo) + _dot_nt(wdtT_lo, h_bf)) + _dot_nt(wdtT_hi, h_bf)
    v_bf = proj(OFF_I, HG_W).astype(BF16)
    g_raw = proj(OFF_G, HG_W)
    z_raw = proj(OFF_Z, M2_W)

    logits = lb_logits_ref[...]
    mx = jnp.max(logits, axis=0, keepdims=True)
    ez = jnp.exp(logits - mx)
    lb = jnp.sum(ez[0:layer + 1], axis=0, keepdims=True) / jnp.sum(ez, axis=0, keepdims=True)
    one_m_lb = 1.0 - lb

    sg = _sigmoid(fl)
    f = lb + one_m_lb * sg
    k = one_m_lb * (1.0 - sg)
    b2 = _chunk_cumsum(jnp.log2(f), HG_CHUNK)
    e = jnp.exp2(b2)
    q = _silu(q_raw)
    qt = q * e
    qt_bf = qt.astype(BF16)
    kt_bf = (k / e).astype(BF16)
    c0 = slice(0, HG_CHUNK)
    c1 = slice(HG_CHUNK, 2 * HG_CHUNK)
    tot0 = b2[HG_CHUNK - 1:HG_CHUNK, :]
    tot1 = b2[2 * HG_CHUNK - 1:2 * HG_CHUNK, :]
    khat0 = k[c0, :] * jnp.exp2(tot0 - b2[c0, :])
    khat1 = k[c1, :] * jnp.exp2(tot1 - b2[c1, :])
    khat0_bf = khat0.astype(BF16)
    kend_bf = jnp.concatenate([(khat0 * jnp.exp2(tot1)).astype(BF16), khat1.astype(BF16)], axis=0)
    qg_bf = jnp.concatenate([qt_bf[c0, :], (qt[c1, :] * jnp.exp2(tot0)).astype(BF16)], axis=0)
    keys1_bf = jnp.concatenate([khat0_bf, kt_bf[c1, :]], axis=0)
    e_tile = jnp.exp2(tot0 + tot1)
    same_chunk_safe = jnp.min(jnp.minimum(tot0, tot1)) >= HG_SAFE_LOG2_DECAY

    ubuf_ref[CONV_PAD:CONV_PAD + tl, :] = u
    conv = u * conv_w_ref[M2_CONV - 1:M2_CONV, :]
    for j in range(M2_CONV - 1):
        sh = M2_CONV - 1 - j
        conv = conv + ubuf_ref[CONV_PAD - sh:CONV_PAD - sh + tl, :] * conv_w_ref[j:j + 1, :]
    ubuf_ref[0:CONV_PAD, :] = ubuf_ref[tl:tl + CONV_PAD, :]
    xbc = _silu(conv + conv_b_ref[...])
    xs = xbc[:, 0:M2_W]
    bm_bf = xbc[:, M2_W:M2_W + M2_GROUPS * M2_STATE].astype(BF16)
    cm_bf = xbc[:, M2_W + M2_GROUPS * M2_STATE:].astype(BF16)

    dtT = _softplus(dt_rawT + dt_bias_col_ref[...])
    aT = dtT * (jnp.exp(a_log_col_ref[...]) * (-LOG2E))
    csT = _dot_exact_right(aT, _block_tril(tl, M2_CHUNK, upper=True))
    dt = dtT.T
    cs = csT.T

    hr, hc = _iota2((M2_HEADS, M2_W))
    expand = jnp.where(hc // M2_HEADDIM == hr, 1.0, 0.0).astype(BF16)
    r2, c2 = _iota2((M2_CHUNK, M2_CHUNK))
    causal_m2 = c2 <= r2
    qr, qc = _iota2((M2_QUAD * M2_CHUNK, M2_QUAD * M2_HEADDIM))
    quad_mask = (qr // M2_CHUNK) == (qc // M2_HEADDIM)

    rc, cc = _iota2((HG_CHUNK, HG_CHUNK))
    causal0 = cc <= rc
    r1, k1 = _iota2((HG_CHUNK, tl))
    causal1 = k1 - HG_CHUNK <= r1

    def head_cols(hd):
        return slice(hd * HG_DK, (hd + 1) * HG_DK)

    o_inter = []
    for p in range(HG_HEADS // 2):
        st_pair = _block_diag2(hg_state_ref[2 * p].astype(BF16), hg_state_ref[2 * p + 1].astype(BF16))
        oi = _dot_nt(qg_bf[:, 2 * p * HG_DK:(2 * p + 2) * HG_DK], st_pair)
        o_inter += [oi[:, 0:HG_DV], oi[:, HG_DV:2 * HG_DV]]
    att0, att1 = [], []
    for hd in range(HG_HEADS):
        cols = head_cols(hd)
        att0.append(jnp.where(causal0, _dot_nt(qt_bf[c0, cols], kt_bf[c0, cols]), 0.0).astype(BF16))
        att1.append(jnp.where(causal1, _dot_nt(qt_bf[c1, cols], keys1_bf[:, cols]), 0.0).astype(BF16))
        hg_state_ref[hd] = hg_state_ref[hd] * e_tile[:, cols] + _dot_tn(v_bf[:, cols], kend_bf[:, cols])

    nck = tl // M2_CHUNK
    rows_of = lambda c: slice(c * M2_CHUNK, (c + 1) * M2_CHUNK)
    ncols_of = lambda gi: slice(gi * M2_STATE, (gi + 1) * M2_STATE)
    gcols_of = lambda gi: slice(gi * M2_GW, (gi + 1) * M2_GW)
    xs_bf = xs.astype(BF16)
    cb = {(c, gi): _dot_nt(cm_bf[rows_of(c), ncols_of(gi)], bm_bf[rows_of(c), ncols_of(gi)])
          for c in range(nck) for gi in range(M2_GROUPS)}
    ecs_x, xdec_bf = [], []
    for c in range(nck):
        cs_c = cs[rows_of(c), :]
        cs_last = cs_c[M2_CHUNK - 1:M2_CHUNK, :]
        ecs_x.append(_dot(jnp.exp2(cs_c).astype(BF16), expand))
        dec_x = _dot((jnp.exp2(cs_last - cs_c) * dt[rows_of(c), :]).astype(BF16), expand)
        xdec_bf.append((xs[rows_of(c), :] * dec_x).astype(BF16))
    y_off = {}
    for gi in range(M2_GROUPS):
        stT = m2_state_ref[gi]
        for c in range(nck):
            cg = cm_bf[rows_of(c), ncols_of(gi)]
            y_off[c, gi] = _dot(cg, stT.astype(BF16)) * ecs_x[c][:, gcols_of(gi)]
            e_last_x = ecs_x[c][M2_CHUNK - 1:M2_CHUNK, gcols_of(gi)]
            stT = stT * e_last_x + _dot_tn(bm_bf[rows_of(c), ncols_of(gi)], xdec_bf[c][:, gcols_of(gi)])
        m2_state_ref[gi] = stT

    for hd in range(HG_HEADS):
        cols = head_cols(hd)
        vh = v_bf[:, cols]
        o_hg_ref[c0, cols] = _rms_rows(o_inter[hd][c0, :] + _dot(att0[hd], vh[c0, :]))
        o_hg_ref[c1, cols] = _rms_rows(o_inter[hd][c1, :] + _dot(att1[hd], vh))

    for c in range(nck):
        rows = rows_of(c)
        for gi in range(M2_GROUPS):
            for qd in range(M2_HPG // M2_QUAD):
                ws = []
                for hh in range(M2_QUAD):
                    hidx = gi * M2_HPG + qd * M2_QUAD + hh
                    seg = cs[rows, hidx:hidx + 1] - csT[hidx:hidx + 1, rows]
                    lm = jnp.exp2(jnp.where(causal_m2, seg, -jnp.inf))
                    ws.append(((cb[c, gi] * lm) * dtT[hidx:hidx + 1, rows]).astype(BF16))
                q0 = (gi * M2_HPG + qd * M2_QUAD) * M2_HEADDIM
                xq = xs_bf[rows, q0:q0 + M2_QUAD * M2_HEADDIM]
                x_bd = jnp.where(quad_mask, jnp.concatenate([xq] * M2_QUAD, axis=0), jnp.zeros_like(quad_mask, BF16))
                y_diag = _dot(jnp.concatenate(ws, axis=1), x_bd)
                o0 = qd * M2_QUAD * M2_HEADDIM
                y_m2_ref[rows, q0:q0 + M2_QUAD * M2_HEADDIM] = y_diag + y_off[c, gi][:, o0:o0 + M2_QUAD * M2_HEADDIM]

    y = y_m2_ref[...] + d_skip_ref[...] * xs
    y = y * _silu(z_raw)
    mix_m2 = (jnp.concatenate([_rms_rows(y[:, gi * M2_GW:(gi + 1) * M2_GW]) for gi in range(M2_GROUPS)], axis=-1)
              * m2_norm_w_ref[...]).astype(BF16)
    out_m2 = _dot(mix_m2, w_out_ref[HG_W:HG_W + M2_W, :])

    @pl.when(jnp.logical_not(same_chunk_safe))
    def _():
        same = _hg_same_chunk_exact(q, k, b2, tl)
        for hd in range(HG_HEADS):
            cols = head_cols(hd)
            vh = v_bf[:, cols]
            cross = _dot_nt(qt_bf[c1, cols], khat0_bf[:, cols])
            o0 = o_inter[hd][c0, :] + _dot(same[hd][c0, c0].astype(BF16), vh[c0, :])
            o1 = (o_inter[hd][c1, :] + _dot(cross.astype(BF16), vh[c0, :])
                  + _dot(same[hd][c1, c1].astype(BF16), vh[c1, :]))
            o_hg_ref[c0, cols] = _rms_rows(o0)
            o_hg_ref[c1, cols] = _rms_rows(o1)

    mix_hg = (o_hg_ref[...] * hg_norm_w_ref[...] * _silu(g_raw)).astype(BF16)

    xr = x + (_dot(mix_hg, w_out_ref[0:HG_W, :]) + out_m2)
    if apply_final_norm:
        ms2 = jnp.mean(xr * xr, axis=-1, keepdims=True)
        xr = (xr * lax.rsqrt(ms2 + EPS)) * final_norm_w_ref[...]
    o_ref[0] = xr.astype(o_ref.dtype)


def _const_spec(shape):
    nd = len(shape)
    return pl.BlockSpec(shape, lambda b, l: (0,) * nd, pipeline_mode=pl.Buffered(1))


def _layer(x, layer, depth, norm_w, w_in, hg_lb_logits, hg_norm_w, conv_w, conv_b, dt_bias, a_log,
           d_skip, m2_norm_w, w_out, final_norm_w):
    bsz, seq, d = x.shape
    assert d == D_MODEL and seq % SEQ_TILE == 0
    assert w_in.shape == (D_MODEL, MAIN_COLS + M2_HEADS)
    tl = SEQ_TILE

    wdt = w_in[:, MAIN_COLS:]
    wdt_hi = wdt.astype(BF16)
    wdt_lo = (wdt - wdt_hi.astype(F32)).astype(BF16)
    row = lambda t: t.reshape(1, -1).astype(F32)
    col = lambda t: t.reshape(-1, 1).astype(F32)

    operands = (
        x, row(norm_w), w_in.astype(F32).T, wdt_hi.T, wdt_lo.T,
        hg_lb_logits.astype(F32), row(hg_norm_w), conv_w.astype(F32), row(conv_b),
        col(dt_bias), col(a_log),
        row(jnp.repeat(d_skip, M2_HEADDIM)), row(m2_norm_w), w_out.astype(F32), row(final_norm_w),
    )
    hbm_operands = (2, 13)
    in_specs = [pl.BlockSpec((1, tl, d), lambda b, l: (b, l, 0))]
    in_specs += [pl.BlockSpec(memory_space=pl.ANY) if i in hbm_operands else _const_spec(op.shape)
                 for i, op in enumerate(operands) if i > 0]

    kern = functools.partial(_layer_kernel, layer=layer, apply_final_norm=(layer == depth - 1))
    return pl.pallas_call(
        kern,
        grid=(bsz, seq // tl),
        in_specs=in_specs,
        out_specs=pl.BlockSpec((1, tl, d), lambda b, l: (b, l, 0)),
        out_shape=jax.ShapeDtypeStruct(x.shape, x.dtype),
        scratch_shapes=[
            pltpu.VMEM((HG_HEADS, HG_DV, HG_DK), F32),
            pltpu.VMEM((M2_GROUPS, M2_STATE, M2_GW), F32),
            pltpu.VMEM((CONV_PAD + tl, M2_CONV_DIM), F32),
            pltpu.VMEM((tl, HG_W), F32),
            pltpu.VMEM((tl, M2_W), F32),
            pltpu.VMEM((D_MODEL, MAIN_COLS), BF16),
            pltpu.VMEM(w_out.shape, BF16),
            pltpu.VMEM((2, W_IN_STAGE_ROWS, D_MODEL), F32),
            pltpu.VMEM((2, W_OUT_STAGE_ROWS, w_out.shape[1]), F32),
            pltpu.SemaphoreType.DMA((2,)),
            pltpu.SemaphoreType.DMA((2,)),
        ],
        compiler_params=pltpu.CompilerParams(
            dimension_semantics=("arbitrary", "arbitrary"),
            vmem_limit_bytes=VMEM_LIMIT_BYTES,
        ),
        name="hybrid_layer",
    )(*operands)


@jax.jit
def kernel(x, norm_w, w_in, hg_lb_logits, hg_norm_w, m2_conv_w, m2_conv_b, m2_dt_bias, m2_a_log,
           m2_d_skip, m2_norm_w, w_out, final_norm_w):
    depth = w_in.shape[0]
    for l in range(depth):
        x = _layer(x, l, depth, norm_w[l], w_in[l], hg_lb_logits, hg_norm_w[l], m2_conv_w[l],
                   m2_conv_b[l], m2_dt_bias[l], m2_a_log[l], m2_d_skip[l], m2_norm_w[l], w_out[l],
                   final_norm_w)
    return x
```

```python
import functools
import math

import jax
import jax.numpy as jnp
from jax import lax
from jax.experimental import pallas as pl
from jax.experimental.pallas import tpu as pltpu

F32 = jnp.float32
BF16 = jnp.bfloat16

EPS = 1e-6
LOG2E = math.log2(math.e)
D_MODEL = 1024
HG_HEADS = 8
HG_DK = 128
HG_DV = 128
HG_W = HG_HEADS * HG_DK
HG_CHUNK = 128
M2_HEADDIM = 64
M2_W = 1024
M2_HEADS = M2_W // M2_HEADDIM
M2_GROUPS = 2
M2_STATE = 128
M2_CONV = 4
M2_CHUNK = 128
M2_CONV_DIM = M2_W + 2 * M2_GROUPS * M2_STATE
M2_GW = M2_W // M2_GROUPS
M2_HPG = M2_HEADS // M2_GROUPS
M2_QUAD = 4

OFF_Q = 0
OFF_F = OFF_Q + HG_W
OFF_I = OFF_F + HG_W
OFF_G = OFF_I + HG_W
OFF_Z = OFF_G + HG_W
OFF_XBC = OFF_Z + M2_W
OFF_DT = OFF_XBC + M2_CONV_DIM
MAIN_COLS = OFF_DT

SEQ_TILE = 2 * HG_CHUNK
STREAMS = 1
CONV_PAD = 8
W_IN_STAGE_ROWS = 128
W_OUT_STAGE_ROWS = 256
VMEM_LIMIT_BYTES = 58 * 1024 * 1024
HG_SAFE_LOG2_DECAY = -108.0


def _dot(a, b):
    return jnp.dot(a, b, preferred_element_type=F32)


def _dot_nt(a, b):
    return lax.dot_general(a, b, (((1,), (1,)), ((), ())), preferred_element_type=F32)


def _dot_tn(a, b):
    return lax.dot_general(a, b, (((0,), (0,)), ((), ())), preferred_element_type=F32)


def _split3(x):
    hi = x.astype(BF16)
    r1 = x - hi.astype(F32)
    mid = r1.astype(BF16)
    lo = (r1 - mid.astype(F32)).astype(BF16)
    return hi, mid, lo


def _dot_exact_left(mat_bf16, x):
    hi, mid, lo = _split3(x)
    return (_dot(mat_bf16, lo) + _dot(mat_bf16, mid)) + _dot(mat_bf16, hi)


def _dot_exact_right(x, mat_bf16):
    hi, mid, lo = _split3(x)
    return (_dot(lo, mat_bf16) + _dot(mid, mat_bf16)) + _dot(hi, mat_bf16)


def _sigmoid(x):
    return 1.0 / (1.0 + jnp.exp2(x * (-LOG2E)))


def _silu(x):
    return x * _sigmoid(x)


def _softplus(x):
    return jnp.maximum(x, 0.0) + jnp.log(1.0 + jnp.exp(-jnp.abs(x)))


def _iota2(shape):
    return lax.broadcasted_iota(jnp.int32, shape, 0), lax.broadcasted_iota(jnp.int32, shape, 1)


def _block_tril(n, block, upper=False):
    r, c = _iota2((n, n))
    same = (r // block) == (c // block)
    tri = (r <= c) if upper else (c <= r)
    return jnp.where(same & tri, 1.0, 0.0).astype(BF16)


def _block_diag2(a, b):
    z = jnp.zeros_like(a)
    return jnp.concatenate([jnp.concatenate([a, z], axis=1), jnp.concatenate([z, b], axis=1)], axis=0)


def _chunk_cumsum(x, chunk):
    n, w = x.shape
    sub = lax.broadcasted_iota(jnp.int32, (8, w), 0)
    out = []
    run = None
    for g in range(n // 8):
        blk = x[8 * g:8 * g + 8, :]
        for d in (1, 2, 4):
            blk = blk + jnp.where(sub >= d, pltpu.roll(blk, d, axis=0), 0.0)
        if (8 * g) % chunk != 0:
            blk = blk + run
        run = jnp.broadcast_to(blk[7:8, :], (8, w))
        out.append(blk)
    return jnp.concatenate(out, axis=0)


def _rms_rows(o):
    return o * lax.rsqrt(jnp.mean(o * o, axis=-1, keepdims=True) + EPS)


def _hg_same_chunk_exact(q, k, b2, tl):
    r, c = _iota2((tl, tl))
    q_bf = q.astype(BF16)
    k_bf = k.astype(BF16)
    atts = [jnp.where(r == c, _dot_nt(q_bf[:, hd * HG_DK:(hd + 1) * HG_DK], k_bf[:, hd * HG_DK:(hd + 1) * HG_DK]), 0.0)
            for hd in range(HG_HEADS)]
    row = lax.broadcasted_iota(jnp.int32, (tl, 1), 0)
    h = HG_CHUNK // 2
    while h >= 1:
        mid = (r // (2 * h)) * (2 * h) + (h - 1)
        sel = jnp.where(c == mid, 1.0, 0.0).astype(BF16)
        ref = _dot_exact_left(sel, b2)
        e = jnp.exp2(-jnp.abs(b2 - ref))
        is_q = (row % (2 * h)) >= h
        zq = jnp.where(is_q, q * e, 0.0).astype(BF16)
        zk = jnp.where(is_q, 0.0, k * e).astype(BF16)
        pair = ((r // (2 * h)) == (c // (2 * h))) & ((r % (2 * h)) >= h) & ((c % (2 * h)) < h)
        for hd in range(HG_HEADS):
            cols = slice(hd * HG_DK, (hd + 1) * HG_DK)
            atts[hd] = atts[hd] + jnp.where(pair, _dot_nt(zq[:, cols], zk[:, cols]), 0.0)
        h //= 2
    return atts


class _Vals(dict):
    __getattr__ = dict.__getitem__
    __setattr__ = dict.__setitem__


def _front(x, p, st, layer):
    tl = x.shape[0]
    v = _Vals(x=x)
    ms = jnp.mean(x * x, axis=-1, keepdims=True)
    rs = lax.rsqrt(ms + EPS)
    xw = x * p.norm_w[...]
    h = xw * rs
    h_bf = h.astype(BF16)
    h_lo = (h - h_bf.astype(F32)).astype(BF16)

    def proj(off, width):
        return _dot(h_bf, p.w_main[:, off:off + width])

    fl = _dot(xw.astype(BF16), p.w_main[:, OFF_F:OFF_F + HG_W]) * rs
    wdtT_hi = p.wdtT_hi[...]
    wdtT_lo = p.wdtT_lo[...]
    dt_rawT = (_dot_nt(wdtT_hi, h_lo) + _dot_nt(wdtT_lo, h_bf)) + _dot_nt(wdtT_hi, h_bf)
    q_raw = proj(OFF_Q, HG_W)
    v.dtT = _softplus(dt_rawT + p.dt_bias_col[...])
    aT = v.dtT * (jnp.exp(p.a_log_col[...]) * (-LOG2E))
    u = proj(OFF_XBC, M2_CONV_DIM)
    v.csT = _dot_exact_right(aT, _block_tril(tl, M2_CHUNK, upper=True))
    v.vT_bf = _dot_nt(p.w_iT[...], h_bf).astype(BF16)
    v.gate_hg = _silu(proj(OFF_G, HG_W)).astype(BF16)
    v.gate_m2 = _silu(proj(OFF_Z, M2_W)).astype(BF16)

    logits = p.lb_logits[...]
    mx = jnp.max(logits, axis=0, keepdims=True)
    ez = jnp.exp(logits - mx)
    lb = jnp.sum(ez[0:layer + 1], axis=0, keepdims=True) / jnp.sum(ez, axis=0, keepdims=True)
    one_m_lb = 1.0 - lb
    sg = _sigmoid(fl)
    f = lb + one_m_lb * sg
    k = one_m_lb * (1.0 - sg)
    b2 = _chunk_cumsum(jnp.log2(f), HG_CHUNK)
    e = jnp.exp2(b2)
    q = _silu(q_raw)
    qt = q * e
    v.q, v.k, v.b2 = q, k, b2
    v.qt_bf = qt.astype(BF16)
    v.kt_bf = (k / e).astype(BF16)
    c0, c1 = _C0, _C1
    tot0 = b2[HG_CHUNK - 1:HG_CHUNK, :]
    tot1 = b2[2 * HG_CHUNK - 1:2 * HG_CHUNK, :]
    khat0 = k[c0, :] * jnp.exp2(tot0 - b2[c0, :])
    khat1 = k[c1, :] * jnp.exp2(tot1 - b2[c1, :])
    v.khat0_bf = khat0.astype(BF16)
    v.kend_bf = jnp.concatenate([(khat0 * jnp.exp2(tot1)).astype(BF16), khat1.astype(BF16)], axis=0)
    v.qg_bf = jnp.concatenate([v.qt_bf[c0, :], (qt[c1, :] * jnp.exp2(tot0)).astype(BF16)], axis=0)
    v.keys1_bf = jnp.concatenate([v.khat0_bf, v.kt_bf[c1, :]], axis=0)
    v.e_tile = jnp.exp2(tot0 + tot1)
    v.same_chunk_safe = jnp.min(jnp.minimum(tot0, tot1)) >= HG_SAFE_LOG2_DECAY

    st.ubuf[CONV_PAD:CONV_PAD + tl, :] = u
    conv = u * p.conv_w[M2_CONV - 1:M2_CONV, :]
    for j in range(M2_CONV - 1):
        sh = M2_CONV - 1 - j
        conv = conv + st.ubuf[CONV_PAD - sh:CONV_PAD - sh + tl, :] * p.conv_w[j:j + 1, :]
    st.ubuf[0:CONV_PAD, :] = st.ubuf[tl:tl + CONV_PAD, :]
    xbc = _silu(conv + p.conv_b[...])
    v.xs = xbc[:, 0:M2_W]
    v.bm_bf = xbc[:, M2_W:M2_W + M2_GROUPS * M2_STATE].astype(BF16)
    v.cm_bf = xbc[:, M2_W + M2_GROUPS * M2_STATE:].astype(BF16)
    v.dt = v.dtT.T
    v.cs = v.csT.T
    return v


def _head_cols(hd):
    return slice(hd * HG_DK, (hd + 1) * HG_DK)


def _store_mix_hg(v, p, st, hd, o0, o1):
    cols = _head_cols(hd)
    for rows, o in ((_C0, o0), (_C1, o1)):
        st.mix_hg[rows, cols] = (_rms_rows(o) * p.hg_norm_w[:, cols] * v.gate_hg[rows, cols]).astype(BF16)


_C0 = slice(0, HG_CHUNK)
_C1 = slice(HG_CHUNK, 2 * HG_CHUNK)


def _waves(v, p, st):
    tl = v.x.shape[0]
    c0, c1 = _C0, _C1
    rc, cc = _iota2((HG_CHUNK, HG_CHUNK))
    causal0 = cc <= rc
    r1, k1 = _iota2((HG_CHUNK, tl))
    causal1 = k1 - HG_CHUNK <= r1
    v.o_inter = []
    for pr in range(HG_HEADS // 2):
        st_pair = _block_diag2(st.hg_state[2 * pr].astype(BF16), st.hg_state[2 * pr + 1].astype(BF16))
        oi = _dot_nt(v.qg_bf[:, 2 * pr * HG_DK:(2 * pr + 2) * HG_DK], st_pair)
        v.o_inter += [oi[:, 0:HG_DV], oi[:, HG_DV:2 * HG_DV]]
    att0, att1 = [], []
    for hd in range(HG_HEADS):
        cols = _head_cols(hd)
        att0.append(jnp.where(causal0, _dot_nt(v.qt_bf[c0, cols], v.kt_bf[c0, cols]), 0.0).astype(BF16))
        att1.append(jnp.where(causal1, _dot_nt(v.qt_bf[c1, cols], v.keys1_bf[:, cols]), 0.0).astype(BF16))
    for hd in range(HG_HEADS):
        cols = _head_cols(hd)
        st.hg_state[hd] = st.hg_state[hd] * v.e_tile[:, cols] + _dot(v.vT_bf[cols, :], v.kend_bf[:, cols])

    hr, hc = _iota2((M2_HEADS, M2_W))
    expand = jnp.where(hc // M2_HEADDIM == hr, 1.0, 0.0).astype(BF16)
    r2, c2 = _iota2((M2_CHUNK, M2_CHUNK))
    causal_m2 = c2 <= r2
    qr, qc = _iota2((M2_QUAD * M2_CHUNK, M2_QUAD * M2_HEADDIM))
    quad_mask = (qr // M2_CHUNK) == (qc // M2_HEADDIM)
    nck = tl // M2_CHUNK
    rows_of = lambda c: slice(c * M2_CHUNK, (c + 1) * M2_CHUNK)
    ncols_of = lambda gi: slice(gi * M2_STATE, (gi + 1) * M2_STATE)
    gcols_of = lambda gi: slice(gi * M2_GW, (gi + 1) * M2_GW)
    xs, cs, csT, dtT = v.xs, v.cs, v.csT, v.dtT
    xs_bf = xs.astype(BF16)
    cb = {(c, gi): _dot_nt(v.cm_bf[rows_of(c), ncols_of(gi)], v.bm_bf[rows_of(c), ncols_of(gi)])
          for c in range(nck) for gi in range(M2_GROUPS)}
    ecs_x, xdec_bf = [], []
    for c in range(nck):
        cs_c = cs[rows_of(c), :]
        cs_last = cs_c[M2_CHUNK - 1:M2_CHUNK, :]
        ecs_x.append(_dot(jnp.exp2(cs_c).astype(BF16), expand))
        dec_x = _dot((jnp.exp2(cs_last - cs_c) * v.dt[rows_of(c), :]).astype(BF16), expand)
        xdec_bf.append((xs[rows_of(c), :] * dec_x).astype(BF16))
    y_off = {}
    for gi in range(M2_GROUPS):
        stT = st.m2_state[gi]
        for c in range(nck):
            cg = v.cm_bf[rows_of(c), ncols_of(gi)]
            y_off[c, gi] = _dot(cg, stT.astype(BF16)) * ecs_x[c][:, gcols_of(gi)]
            e_last_x = ecs_x[c][M2_CHUNK - 1:M2_CHUNK, gcols_of(gi)]
            stT = stT * e_last_x + _dot_tn(v.bm_bf[rows_of(c), ncols_of(gi)], xdec_bf[c][:, gcols_of(gi)])
        st.m2_state[gi] = stT

    for hd in range(HG_HEADS):
        cols = _head_cols(hd)
        vT = v.vT_bf[cols, :]
        _store_mix_hg(v, p, st, hd, v.o_inter[hd][c0, :] + _dot_nt(att0[hd], vT[:, c0]),
                      v.o_inter[hd][c1, :] + _dot_nt(att1[hd], vT))

    y_rows = []
    for c in range(nck):
        rows = rows_of(c)
        y_quads = []
        for gi in range(M2_GROUPS):
            for qd in range(M2_HPG // M2_QUAD):
                ws = []
                for hh in range(M2_QUAD):
                    hidx = gi * M2_HPG + qd * M2_QUAD + hh
                    seg = cs[rows, hidx:hidx + 1] - csT[hidx:hidx + 1, rows]
                    lm = jnp.exp2(jnp.where(causal_m2, seg, -jnp.inf))
                    ws.append(((cb[c, gi] * lm) * dtT[hidx:hidx + 1, rows]).astype(BF16))
                q0 = (gi * M2_HPG + qd * M2_QUAD) * M2_HEADDIM
                xq = xs_bf[rows, q0:q0 + M2_QUAD * M2_HEADDIM]
                x_bd = jnp.where(quad_mask, jnp.concatenate([xq] * M2_QUAD, axis=0), jnp.zeros_like(quad_mask, BF16))
                y_diag = _dot(jnp.concatenate(ws, axis=1), x_bd)
                o0 = qd * M2_QUAD * M2_HEADDIM
                y_quads.append(y_diag + y_off[c, gi][:, o0:o0 + M2_QUAD * M2_HEADDIM])
        y_rows.append(jnp.concatenate(y_quads, axis=1))

    y = jnp.concatenate(y_rows, axis=0) + p.d_skip[...] * xs
    y = y * v.gate_m2
    mix_m2 = (jnp.concatenate([_rms_rows(y[:, gi * M2_GW:(gi + 1) * M2_GW]) for gi in range(M2_GROUPS)], axis=-1)
              * p.m2_norm_w[...]).astype(BF16)
    v.out_m2 = _dot(mix_m2, p.w_out[HG_W:HG_W + M2_W, :])


def _redo_heads_exact(v, p, st):
    c0, c1 = _C0, _C1
    same = _hg_same_chunk_exact(v.q, v.k, v.b2, v.x.shape[0])
    for hd in range(HG_HEADS):
        cols = _head_cols(hd)
        vT = v.vT_bf[cols, :]
        cross = _dot_nt(v.qt_bf[c1, cols], v.khat0_bf[:, cols])
        o0 = v.o_inter[hd][c0, :] + _dot_nt(same[hd][c0, c0].astype(BF16), vT[:, c0])
        o1 = (v.o_inter[hd][c1, :] + _dot_nt(cross.astype(BF16), vT[:, c0])
              + _dot_nt(same[hd][c1, c1].astype(BF16), vT[:, c1]))
        _store_mix_hg(v, p, st, hd, o0, o1)


def _tail(v, p, st, apply_final_norm):
    xr = v.x + (_dot(st.mix_hg[...], p.w_out[0:HG_W, :]) + v.out_m2)
    if apply_final_norm:
        ms2 = jnp.mean(xr * xr, axis=-1, keepdims=True)
        xr = (xr * lax.rsqrt(ms2 + EPS)) * p.final_norm_w[...]
    return xr


def _stream_rows(src_hbm, nrows, stage_ref, sem, consume):
    rc = stage_ref.shape[1]
    nchunk = nrows // rc
    assert nchunk * rc == nrows

    def chunk_copy(i):
        return pltpu.make_async_copy(src_hbm.at[pl.ds(i * rc, rc), :], stage_ref.at[i % 2], sem.at[i % 2])

    chunk_copy(0).start()
    for i in range(nchunk):
        if i + 1 < nchunk:
            chunk_copy(i + 1).start()
        chunk_copy(i).wait()
        consume(i, stage_ref[i % 2])


def _load_weights(w_inT_hbm, w_out_hbm, w_main_ref, w_iT_ref, w_out_ref, stage_in_ref, stage_out_ref, sem_in, sem_out):
    rc = stage_in_ref.shape[1]

    def put_in(i, chunk):
        w_main_ref[:, i * rc:(i + 1) * rc] = chunk.T.astype(BF16)
        lo = i * rc - OFF_I
        if 0 <= lo < HG_W:
            w_iT_ref[lo:lo + rc, :] = chunk.astype(BF16)

    def put_out(i, chunk):
        ro = stage_out_ref.shape[1]
        w_out_ref[i * ro:(i + 1) * ro, :] = chunk.astype(BF16)

    _stream_rows(w_inT_hbm, MAIN_COLS, stage_in_ref, sem_in, put_in)
    _stream_rows(w_out_hbm, w_out_hbm.shape[0], stage_out_ref, sem_out, put_out)


def _layer_kernel(x_ref, norm_w_ref, w_inT_hbm, wdtT_hi_ref, wdtT_lo_ref,
                  lb_logits_ref, hg_norm_w_ref, conv_w_ref, conv_b_ref,
                  dt_bias_col_ref, a_log_col_ref,
                  d_skip_ref, m2_norm_w_ref, w_out_hbm, final_norm_w_ref,
                  o_ref,
                  hg_state_ref, m2_state_ref, ubuf_ref, mix_hg_ref,
                  w_main_ref, w_iT_ref, w_out_ref, stage_in_ref, stage_out_ref, sem_in, sem_out,
                  *, layer, apply_final_norm):
    nstream, tl = x_ref.shape[0], x_ref.shape[1]
    assert tl == 2 * HG_CHUNK and tl % M2_CHUNK == 0
    p = _Vals(norm_w=norm_w_ref, w_main=w_main_ref, w_iT=w_iT_ref, wdtT_hi=wdtT_hi_ref, wdtT_lo=wdtT_lo_ref,
              lb_logits=lb_logits_ref, hg_norm_w=hg_norm_w_ref, conv_w=conv_w_ref, conv_b=conv_b_ref,
              dt_bias_col=dt_bias_col_ref, a_log_col=a_log_col_ref, d_skip=d_skip_ref,
              m2_norm_w=m2_norm_w_ref, w_out=w_out_ref, final_norm_w=final_norm_w_ref)
    sts = [_Vals(hg_state=hg_state_ref.at[s], m2_state=m2_state_ref.at[s], ubuf=ubuf_ref.at[s],
                 mix_hg=mix_hg_ref.at[s]) for s in range(nstream)]

    @pl.when((pl.program_id(0) == 0) & (pl.program_id(1) == 0))
    def _():
        _load_weights(w_inT_hbm, w_out_hbm, w_main_ref, w_iT_ref, w_out_ref,
                      stage_in_ref, stage_out_ref, sem_in, sem_out)

    @pl.when(pl.program_id(1) == 0)
    def _():
        hg_state_ref[...] = jnp.zeros_like(hg_state_ref)
        m2_state_ref[...] = jnp.zeros_like(m2_state_ref)
        for s in range(nstream):
            ubuf_ref[s, 0:CONV_PAD, :] = jnp.zeros((CONV_PAD, M2_CONV_DIM), F32)

    vals = []
    for s in range(nstream):
        vals.append(_front(x_ref[s], p, sts[s], layer))
        _waves(vals[s], p, sts[s])

    for s in range(nstream):
        @pl.when(jnp.logical_not(vals[s].same_chunk_safe))
        def _(s=s):
            _redo_heads_exact(vals[s], p, sts[s])

    for s in range(nstream):
        o_ref[s] = _tail(vals[s], p, sts[s], apply_final_norm).astype(o_ref.dtype)


def _const_spec(shape):
    nd = len(shape)
    return pl.BlockSpec(shape, lambda b, l: (0,) * nd, pipeline_mode=pl.Buffered(1))


def _layer(x, layer, depth, norm_w, w_in, hg_lb_logits, hg_norm_w, conv_w, conv_b, dt_bias, a_log,
           d_skip, m2_norm_w, w_out, final_norm_w):
    bsz, seq, d = x.shape
    assert d == D_MODEL and seq % SEQ_TILE == 0 and bsz % STREAMS == 0
    assert w_in.shape == (D_MODEL, MAIN_COLS + M2_HEADS)
    tl, ns = SEQ_TILE, STREAMS

    wdt = w_in[:, MAIN_COLS:]
    wdt_hi = wdt.astype(BF16)
    wdt_lo = (wdt - wdt_hi.astype(F32)).astype(BF16)
    row = lambda t: t.reshape(1, -1).astype(F32)
    col = lambda t: t.reshape(-1, 1).astype(F32)

    operands = (
        x, row(norm_w), w_in.astype(F32).T, wdt_hi.T, wdt_lo.T,
        hg_lb_logits.astype(F32), row(hg_norm_w), conv_w.astype(F32), row(conv_b),
        col(dt_bias), col(a_log),
        row(jnp.repeat(d_skip, M2_HEADDIM)), row(m2_norm_w), w_out.astype(F32), row(final_norm_w),
    )
    hbm_operands = (2, 13)
    in_specs = [pl.BlockSpec((ns, tl, d), lambda b, l: (b, l, 0))]
    in_specs += [pl.BlockSpec(memory_space=pl.ANY) if i in hbm_operands else _const_spec(op.shape)
                 for i, op in enumerate(operands) if i > 0]

    kern = functools.partial(_layer_kernel, layer=layer, apply_final_norm=(layer == depth - 1))
    return pl.pallas_call(
        kern,
        grid=(bsz // ns, seq // tl),
        in_specs=in_specs,
        out_specs=pl.BlockSpec((ns, tl, d), lambda b, l: (b, l, 0)),
        out_shape=jax.ShapeDtypeStruct(x.shape, x.dtype),
        scratch_shapes=[
            pltpu.VMEM((ns, HG_HEADS, HG_DV, HG_DK), F32),
            pltpu.VMEM((ns, M2_GROUPS, M2_STATE, M2_GW), F32),
            pltpu.VMEM((ns, CONV_PAD + tl, M2_CONV_DIM), F32),
            pltpu.VMEM((ns, tl, HG_W), BF16),
            pltpu.VMEM((D_MODEL, MAIN_COLS), BF16),
            pltpu.VMEM((HG_W, D_MODEL), BF16),
            pltpu.VMEM(w_out.shape, BF16),
            pltpu.VMEM((2, W_IN_STAGE_ROWS, D_MODEL), F32),
            pltpu.VMEM((2, W_OUT_STAGE_ROWS, w_out.shape[1]), F32),
            pltpu.SemaphoreType.DMA((2,)),
            pltpu.SemaphoreType.DMA((2,)),
        ],
        compiler_params=pltpu.CompilerParams(
            dimension_semantics=("arbitrary", "arbitrary"),
            vmem_limit_bytes=VMEM_LIMIT_BYTES,
        ),
        name="hybrid_layer",
    )(*operands)


@jax.jit
def kernel(x, norm_w, w_in, hg_lb_logits, hg_norm_w, m2_conv_w, m2_conv_b, m2_dt_bias, m2_a_log,
           m2_d_skip, m2_norm_w, w_out, final_norm_w):
    depth = w_in.shape[0]
    for l in range(depth):
        x = _layer(x, l, depth, norm_w[l], w_in[l], hg_lb_logits, hg_norm_w[l], m2_conv_w[l],
                   m2_conv_b[l], m2_dt_bias[l], m2_a_log[l], m2_d_skip[l], m2_norm_w[l], w_out[l],
                   final_norm_w)
    return x
```

```python
import functools
import math

import jax
import jax.numpy as jnp
from jax import lax
from jax.experimental import pallas as pl
from jax.experimental.pallas import tpu as pltpu

F32 = jnp.float32
BF16 = jnp.bfloat16

EPS = 1e-6
LOG2E = math.log2(math.e)
D_MODEL = 1024
HG_HEADS = 8
HG_DK = 128
HG_DV = 128
HG_W = HG_HEADS * HG_DK
HG_CHUNK = 128
M2_HEADDIM = 64
M2_W = 1024
M2_HEADS = M2_W // M2_HEADDIM
M2_GROUPS = 2
M2_STATE = 128
M2_CONV = 4
M2_CHUNK = 128
M2_CONV_DIM = M2_W + 2 * M2_GROUPS * M2_STATE
M2_GW = M2_W // M2_GROUPS
M2_HPG = M2_HEADS // M2_GROUPS
M2_QUAD = 4

OFF_Q = 0
OFF_F = OFF_Q + HG_W
OFF_I = OFF_F + HG_W
OFF_G = OFF_I + HG_W
OFF_Z = OFF_G + HG_W
OFF_XBC = OFF_Z + M2_W
OFF_DT = OFF_XBC + M2_CONV_DIM
MAIN_COLS = OFF_DT

SEQ_TILE = 2 * HG_CHUNK
STREAMS = 1
CONV_PAD = 8
W_IN_STAGE_ROWS = 128
W_OUT_STAGE_ROWS = 256
VMEM_LIMIT_BYTES = 58 * 1024 * 1024
HG_SAFE_LOG2_DECAY = -108.0


def _dot(a, b):
    return jnp.dot(a, b, preferred_element_type=F32)


def _dot_nt(a, b):
    return lax.dot_general(a, b, (((1,), (1,)), ((), ())), preferred_element_type=F32)


def _dot_tn(a, b):
    return lax.dot_general(a, b, (((0,), (0,)), ((), ())), preferred_element_type=F32)


def _split3(x):
    hi = x.astype(BF16)
    r1 = x - hi.astype(F32)
    mid = r1.astype(BF16)
    lo = (r1 - mid.astype(F32)).astype(BF16)
    return hi, mid, lo


def _dot_exact_left(mat_bf16, x):
    hi, mid, lo = _split3(x)
    return (_dot(mat_bf16, lo) + _dot(mat_bf16, mid)) + _dot(mat_bf16, hi)


def _dot_exact_right(x, mat_bf16):
    hi, mid, lo = _split3(x)
    return (_dot(lo, mat_bf16) + _dot(mid, mat_bf16)) + _dot(hi, mat_bf16)


def _sigmoid(x):
    return 1.0 / (1.0 + jnp.exp2(x * (-LOG2E)))


def _silu(x):
    return x * _sigmoid(x)


def _softplus(x):
    return jnp.maximum(x, 0.0) + jnp.log(1.0 + jnp.exp(-jnp.abs(x)))


def _iota2(shape):
    return lax.broadcasted_iota(jnp.int32, shape, 0), lax.broadcasted_iota(jnp.int32, shape, 1)


def _block_tril(n, block, upper=False):
    r, c = _iota2((n, n))
    same = (r // block) == (c // block)
    tri = (r <= c) if upper else (c <= r)
    return jnp.where(same & tri, 1.0, 0.0).astype(BF16)


def _block_diag2(a, b):
    z = jnp.zeros_like(a)
    return jnp.concatenate([jnp.concatenate([a, z], axis=1), jnp.concatenate([z, b], axis=1)], axis=0)


def _chunk_cumsum(x, chunk):
    n, w = x.shape
    sub = lax.broadcasted_iota(jnp.int32, (8, w), 0)
    out = []
    run = None
    for g in range(n // 8):
        blk = x[8 * g:8 * g + 8, :]
        for d in (1, 2, 4):
            blk = blk + jnp.where(sub >= d, pltpu.roll(blk, d, axis=0), 0.0)
        if (8 * g) % chunk != 0:
            blk = blk + run
        run = jnp.broadcast_to(blk[7:8, :], (8, w))
        out.append(blk)
    return jnp.concatenate(out, axis=0)


def _rms_rows(o):
    return o * lax.rsqrt(jnp.mean(o * o, axis=-1, keepdims=True) + EPS)


def _hg_same_chunk_exact(q, k, b2, tl):
    r, c = _iota2((tl, tl))
    q_bf = q.astype(BF16)
    k_bf = k.astype(BF16)
    atts = [jnp.where(r == c, _dot_nt(q_bf[:, hd * HG_DK:(hd + 1) * HG_DK], k_bf[:, hd * HG_DK:(hd + 1) * HG_DK]), 0.0)
            for hd in range(HG_HEADS)]
    row = lax.broadcasted_iota(jnp.int32, (tl, 1), 0)
    h = HG_CHUNK // 2
    while h >= 1:
        mid = (r // (2 * h)) * (2 * h) + (h - 1)
        sel = jnp.where(c == mid, 1.0, 0.0).astype(BF16)
        ref = _dot_exact_left(sel, b2)
        e = jnp.exp2(-jnp.abs(b2 - ref))
        is_q = (row % (2 * h)) >= h
        zq = jnp.where(is_q, q * e, 0.0).astype(BF16)
        zk = jnp.where(is_q, 0.0, k * e).astype(BF16)
        pair = ((r // (2 * h)) == (c // (2 * h))) & ((r % (2 * h)) >= h) & ((c % (2 * h)) < h)
        for hd in range(HG_HEADS):
            cols = slice(hd * HG_DK, (hd + 1) * HG_DK)
            atts[hd] = atts[hd] + jnp.where(pair, _dot_nt(zq[:, cols], zk[:, cols]), 0.0)
        h //= 2
    return atts


class _Vals(dict):
    __getattr__ = dict.__getitem__
    __setattr__ = dict.__setitem__


def _front(x, p, st, layer):
    tl = x.shape[0]
    v = _Vals(x=x)
    ms = jnp.mean(x * x, axis=-1, keepdims=True)
    h = (x * lax.rsqrt(ms + EPS)) * p.norm_w[...]
    h_bf = h.astype(BF16)
    h_lo = (h - h_bf.astype(F32)).astype(BF16)

    def proj(off, width):
        return _dot(h_bf, p.w_main[:, off:off + width])

    fl = proj(OFF_F, HG_W)
    wdtT_hi = p.wdtT_hi[...]
    wdtT_lo = p.wdtT_lo[...]
    dt_rawT = (_dot_nt(wdtT_hi, h_lo) + _dot_nt(wdtT_lo, h_bf)) + _dot_nt(wdtT_hi, h_bf)
    q_raw = proj(OFF_Q, HG_W)
    v.dtT = _softplus(dt_rawT + p.dt_bias_col[...])
    aT = v.dtT * (jnp.exp(p.a_log_col[...]) * (-LOG2E))
    u = proj(OFF_XBC, M2_CONV_DIM)
    v.csT = _dot_exact_right(aT, _block_tril(tl, M2_CHUNK, upper=True))
    v.vT_bf = _dot_nt(p.w_iT[...], h_bf).astype(BF16)
    v.proj = proj

    logits = p.lb_logits[...]
    mx = jnp.max(logits, axis=0, keepdims=True)
    ez = jnp.exp(logits - mx)
    lb = jnp.sum(ez[0:layer + 1], axis=0, keepdims=True) / jnp.sum(ez, axis=0, keepdims=True)
    one_m_lb = 1.0 - lb
    sg = _sigmoid(fl)
    f = lb + one_m_lb * sg
    k = one_m_lb * (1.0 - sg)
    b2 = _chunk_cumsum(jnp.log2(f), HG_CHUNK)
    e = jnp.exp2(b2)
    q = _silu(q_raw)
    qt = q * e
    v.q, v.k, v.b2 = q, k, b2
    v.qt_bf = qt.astype(BF16)
    v.kt_bf = (k / e).astype(BF16)
    c0, c1 = _C0, _C1
    tot0 = b2[HG_CHUNK - 1:HG_CHUNK, :]
    tot1 = b2[2 * HG_CHUNK - 1:2 * HG_CHUNK, :]
    khat0 = k[c0, :] * jnp.exp2(tot0 - b2[c0, :])
    khat1 = k[c1, :] * jnp.exp2(tot1 - b2[c1, :])
    v.khat0_bf = khat0.astype(BF16)
    v.kend_bf = jnp.concatenate([(khat0 * jnp.exp2(tot1)).astype(BF16), khat1.astype(BF16)], axis=0)
    v.qg_bf = jnp.concatenate([v.qt_bf[c0, :], (qt[c1, :] * jnp.exp2(tot0)).astype(BF16)], axis=0)
    v.keys1_bf = jnp.concatenate([v.khat0_bf, v.kt_bf[c1, :]], axis=0)
    v.e_tile = jnp.exp2(tot0 + tot1)
    v.same_chunk_safe = jnp.min(jnp.minimum(tot0, tot1)) >= HG_SAFE_LOG2_DECAY

    st.ubuf[CONV_PAD:CONV_PAD + tl, :] = u
    conv = u * p.conv_w[M2_CONV - 1:M2_CONV, :]
    for j in range(M2_CONV - 1):
        sh = M2_CONV - 1 - j
        conv = conv + st.ubuf[CONV_PAD - sh:CONV_PAD - sh + tl, :] * p.conv_w[j:j + 1, :]
    st.ubuf[0:CONV_PAD, :] = st.ubuf[tl:tl + CONV_PAD, :]
    xbc = _silu(conv + p.conv_b[...])
    v.xs = xbc[:, 0:M2_W]
    v.bm_bf = xbc[:, M2_W:M2_W + M2_GROUPS * M2_STATE].astype(BF16)
    v.cm_bf = xbc[:, M2_W + M2_GROUPS * M2_STATE:].astype(BF16)
    v.dt = v.dtT.T
    v.cs = v.csT.T
    return v


def _head_cols(hd):
    return slice(hd * HG_DK, (hd + 1) * HG_DK)


def _store_mix_hg(v, p, st, hd, o0, o1):
    cols = _head_cols(hd)
    for rows, o in ((_C0, o0), (_C1, o1)):
        st.mix_hg[rows, cols] = (_rms_rows(o) * p.hg_norm_w[:, cols] * v.gate_hg[rows, cols]).astype(BF16)


_C0 = slice(0, HG_CHUNK)
_C1 = slice(HG_CHUNK, 2 * HG_CHUNK)


def _waves(v, p, st):
    tl = v.x.shape[0]
    c0, c1 = _C0, _C1
    rc, cc = _iota2((HG_CHUNK, HG_CHUNK))
    causal0 = cc <= rc
    r1, k1 = _iota2((HG_CHUNK, tl))
    causal1 = k1 - HG_CHUNK <= r1
    v.o_inter = []
    for pr in range(HG_HEADS // 2):
        st_pair = _block_diag2(st.hg_state[2 * pr].astype(BF16), st.hg_state[2 * pr + 1].astype(BF16))
        oi = _dot_nt(v.qg_bf[:, 2 * pr * HG_DK:(2 * pr + 2) * HG_DK], st_pair)
        v.o_inter += [oi[:, 0:HG_DV], oi[:, HG_DV:2 * HG_DV]]
    att0, att1 = [], []
    for hd in range(HG_HEADS):
        cols = _head_cols(hd)
        att0.append(jnp.where(causal0, _dot_nt(v.qt_bf[c0, cols], v.kt_bf[c0, cols]), 0.0).astype(BF16))
        att1.append(jnp.where(causal1, _dot_nt(v.qt_bf[c1, cols], v.keys1_bf[:, cols]), 0.0).astype(BF16))
    for hd in range(HG_HEADS):
        cols = _head_cols(hd)
        st.hg_state[hd] = st.hg_state[hd] * v.e_tile[:, cols] + _dot(v.vT_bf[cols, :], v.kend_bf[:, cols])

    v.gate_hg = _silu(v.proj(OFF_G, HG_W)).astype(BF16)
    v.gate_m2 = _silu(v.proj(OFF_Z, M2_W)).astype(BF16)

    hr, hc = _iota2((M2_HEADS, M2_W))
    expand = jnp.where(hc // M2_HEADDIM == hr, 1.0, 0.0).astype(BF16)
    r2, c2 = _iota2((M2_CHUNK, M2_CHUNK))
    causal_m2 = c2 <= r2
    qr, qc = _iota2((M2_QUAD * M2_CHUNK, M2_QUAD * M2_HEADDIM))
    quad_mask = (qr // M2_CHUNK) == (qc // M2_HEADDIM)
    nck = tl // M2_CHUNK
    rows_of = lambda c: slice(c * M2_CHUNK, (c + 1) * M2_CHUNK)
    ncols_of = lambda gi: slice(gi * M2_STATE, (gi + 1) * M2_STATE)
    gcols_of = lambda gi: slice(gi * M2_GW, (gi + 1) * M2_GW)
    xs, cs, csT, dtT = v.xs, v.cs, v.csT, v.dtT
    xs_bf = xs.astype(BF16)
    cb = {(c, gi): _dot_nt(v.cm_bf[rows_of(c), ncols_of(gi)], v.bm_bf[rows_of(c), ncols_of(gi)])
          for c in range(nck) for gi in range(M2_GROUPS)}
    ecs_x, xdec_bf = [], []
    for c in range(nck):
        cs_c = cs[rows_of(c), :]
        cs_last = cs_c[M2_CHUNK - 1:M2_CHUNK, :]
        ecs_x.append(_dot(jnp.exp2(cs_c).astype(BF16), expand))
        dec_x = _dot((jnp.exp2(cs_last - cs_c) * v.dt[rows_of(c), :]).astype(BF16), expand)
        xdec_bf.append((xs[rows_of(c), :] * dec_x).astype(BF16))
    y_off = {}
    for gi in range(M2_GROUPS):
        stT = st.m2_state[gi]
        for c in range(nck):
            cg = v.cm_bf[rows_of(c), ncols_of(gi)]
            y_off[c, gi] = _dot(cg, stT.astype(BF16)) * ecs_x[c][:, gcols_of(gi)]
            e_last_x = ecs_x[c][M2_CHUNK - 1:M2_CHUNK, gcols_of(gi)]
            stT = stT * e_last_x + _dot_tn(v.bm_bf[rows_of(c), ncols_of(gi)], xdec_bf[c][:, gcols_of(gi)])
        st.m2_state[gi] = stT

    for hd in range(HG_HEADS):
        cols = _head_cols(hd)
        vT = v.vT_bf[cols, :]
        _store_mix_hg(v, p, st, hd, v.o_inter[hd][c0, :] + _dot_nt(att0[hd], vT[:, c0]),
                      v.o_inter[hd][c1, :] + _dot_nt(att1[hd], vT))

    y_rows = []
    for c in range(nck):
        rows = rows_of(c)
        y_quads = []
        for gi in range(M2_GROUPS):
            for qd in range(M2_HPG // M2_QUAD):
                ws = []
                for hh in range(M2_QUAD):
                    hidx = gi * M2_HPG + qd * M2_QUAD + hh
                    seg = cs[rows, hidx:hidx + 1] - csT[hidx:hidx + 1, rows]
                    lm = jnp.exp2(jnp.where(causal_m2, seg, -jnp.inf))
                    ws.append(((cb[c, gi] * lm) * dtT[hidx:hidx + 1, rows]).astype(BF16))
                q0 = (gi * M2_HPG + qd * M2_QUAD) * M2_HEADDIM
                xq = xs_bf[rows, q0:q0 + M2_QUAD * M2_HEADDIM]
                x_bd = jnp.where(quad_mask, jnp.concatenate([xq] * M2_QUAD, axis=0), jnp.zeros_like(quad_mask, BF16))
                y_diag = _dot(jnp.concatenate(ws, axis=1), x_bd)
                o0 = qd * M2_QUAD * M2_HEADDIM
                y_quads.append(y_diag + y_off[c, gi][:, o0:o0 + M2_QUAD * M2_HEADDIM])
        y_rows.append(jnp.concatenate(y_quads, axis=1))

    y = jnp.concatenate(y_rows, axis=0) + p.d_skip[...] * xs
    y = y * v.gate_m2
    mix_m2 = (jnp.concatenate([_rms_rows(y[:, gi * M2_GW:(gi + 1) * M2_GW]) for gi in range(M2_GROUPS)], axis=-1)
              * p.m2_norm_w[...]).astype(BF16)
    v.out_m2 = _dot(mix_m2, p.w_out[HG_W:HG_W + M2_W, :])


def _redo_heads_exact(v, p, st):
    c0, c1 = _C0, _C1
    same = _hg_same_chunk_exact(v.q, v.k, v.b2, v.x.shape[0])
    for hd in range(HG_HEADS):
        cols = _head_cols(hd)
        vT = v.vT_bf[cols, :]
        cross = _dot_nt(v.qt_bf[c1, cols], v.khat0_bf[:, cols])
        o0 = v.o_inter[hd][c0, :] + _dot_nt(same[hd][c0, c0].astype(BF16), vT[:, c0])
        o1 = (v.o_inter[hd][c1, :] + _dot_nt(cross.astype(BF16), vT[:, c0])
              + _dot_nt(same[hd][c1, c1].astype(BF16), vT[:, c1]))
        _store_mix_hg(v, p, st, hd, o0, o1)


def _tail(v, p, st, apply_final_norm):
    xr = v.x + (_dot(st.mix_hg[...], p.w_out[0:HG_W, :]) + v.out_m2)
    if apply_final_norm:
        ms2 = jnp.mean(xr * xr, axis=-1, keepdims=True)
        xr = (xr * lax.rsqrt(ms2 + EPS)) * p.final_norm_w[...]
    return xr


def _stream_rows(src_hbm, nrows, stage_ref, sem, consume):
    rc = stage_ref.shape[1]
    nchunk = nrows // rc
    assert nchunk * rc == nrows

    def chunk_copy(i):
        return pltpu.make_async_copy(src_hbm.at[pl.ds(i * rc, rc), :], stage_ref.at[i % 2], sem.at[i % 2])

    chunk_copy(0).start()
    for i in range(nchunk):
        if i + 1 < nchunk:
            chunk_copy(i + 1).start()
        chunk_copy(i).wait()
        consume(i, stage_ref[i % 2])


def _load_weights(w_inT_hbm, w_out_hbm, w_main_ref, w_iT_ref, w_out_ref, stage_in_ref, stage_out_ref, sem_in, sem_out):
    rc = stage_in_ref.shape[1]

    def put_in(i, chunk):
        w_main_ref[:, i * rc:(i + 1) * rc] = chunk.T.astype(BF16)
        lo = i * rc - OFF_I
        if 0 <= lo < HG_W:
            w_iT_ref[lo:lo + rc, :] = chunk.astype(BF16)

    def put_out(i, chunk):
        ro = stage_out_ref.shape[1]
        w_out_ref[i * ro:(i + 1) * ro, :] = chunk.astype(BF16)

    _stream_rows(w_inT_hbm, MAIN_COLS, stage_in_ref, sem_in, put_in)
    _stream_rows(w_out_hbm, w_out_hbm.shape[0], stage_out_ref, sem_out, put_out)


def _layer_kernel(x_ref, norm_w_ref, w_inT_hbm, wdtT_hi_ref, wdtT_lo_ref,
                  lb_logits_ref, hg_norm_w_ref, conv_w_ref, conv_b_ref,
                  dt_bias_col_ref, a_log_col_ref,
                  d_skip_ref, m2_norm_w_ref, w_out_hbm, final_norm_w_ref,
                  o_ref,
                  hg_state_ref, m2_state_ref, ubuf_ref, mix_hg_ref,
                  w_main_ref, w_iT_ref, w_out_ref, stage_in_ref, stage_out_ref, sem_in, sem_out,
                  *, layer, apply_final_norm):
    nstream, tl = x_ref.shape[0], x_ref.shape[1]
    assert tl == 2 * HG_CHUNK and tl % M2_CHUNK == 0
    p = _Vals(norm_w=norm_w_ref, w_main=w_main_ref, w_iT=w_iT_ref, wdtT_hi=wdtT_hi_ref, wdtT_lo=wdtT_lo_ref,
              lb_logits=lb_logits_ref, hg_norm_w=hg_norm_w_ref, conv_w=conv_w_ref, conv_b=conv_b_ref,
              dt_bias_col=dt_bias_col_ref, a_log_col=a_log_col_ref, d_skip=d_skip_ref,
              m2_norm_w=m2_norm_w_ref, w_out=w_out_ref, final_norm_w=final_norm_w_ref)
    sts = [_Vals(hg_state=hg_state_ref.at[s], m2_state=m2_state_ref.at[s], ubuf=ubuf_ref.at[s],
                 mix_hg=mix_hg_ref.at[s]) for s in range(nstream)]

    @pl.when((pl.program_id(0) == 0) & (pl.program_id(1) == 0))
    def _():
        _load_weights(w_inT_hbm, w_out_hbm, w_main_ref, w_iT_ref, w_out_ref,
                      stage_in_ref, stage_out_ref, sem_in, sem_out)

    @pl.when(pl.program_id(1) == 0)
    def _():
        hg_state_ref[...] = jnp.zeros_like(hg_state_ref)
        m2_state_ref[...] = jnp.zeros_like(m2_state_ref)
        for s in range(nstream):
            ubuf_ref[s, 0:CONV_PAD, :] = jnp.zeros((CONV_PAD, M2_CONV_DIM), F32)

    vals = []
    for s in range(nstream):
        vals.append(_front(x_ref[s], p, sts[s], layer))
        _waves(vals[s], p, sts[s])

    for s in range(nstream):
        @pl.when(jnp.logical_not(vals[s].same_chunk_safe))
        def _(s=s):
            _redo_heads_exact(vals[s], p, sts[s])

    for s in range(nstream):
        o_ref[s] = _tail(vals[s], p, sts[s], apply_final_norm).astype(o_ref.dtype)


def _const_spec(shape):
    nd = len(shape)
    return pl.BlockSpec(shape, lambda b, l: (0,) * nd, pipeline_mode=pl.Buffered(1))


def _layer(x, layer, depth, norm_w, w_in, hg_lb_logits, hg_norm_w, conv_w, conv_b, dt_bias, a_log,
           d_skip, m2_norm_w, w_out, final_norm_w):
    bsz, seq, d = x.shape
    assert d == D_MODEL and seq % SEQ_TILE == 0 and bsz % STREAMS == 0
    assert w_in.shape == (D_MODEL, MAIN_COLS + M2_HEADS)
    tl, ns = SEQ_TILE, STREAMS

    wdt = w_in[:, MAIN_COLS:]
    wdt_hi = wdt.astype(BF16)
    wdt_lo = (wdt - wdt_hi.astype(F32)).astype(BF16)
    row = lambda t: t.reshape(1, -1).astype(F32)
    col = lambda t: t.reshape(-1, 1).astype(F32)

    operands = (
        x, row(norm_w), w_in.astype(F32).T, wdt_hi.T, wdt_lo.T,
        hg_lb_logits.astype(F32), row(hg_norm_w), conv_w.astype(F32), row(conv_b),
        col(dt_bias), col(a_log),
        row(jnp.repeat(d_skip, M2_HEADDIM)), row(m2_norm_w), w_out.astype(F32), row(final_norm_w),
    )
    hbm_operands = (2, 13)
    in_specs = [pl.BlockSpec((ns, tl, d), lambda b, l: (b, l, 0))]
    in_specs += [pl.BlockSpec(memory_space=pl.ANY) if i in hbm_operands else _const_spec(op.shape)
                 for i, op in enumerate(operands) if i > 0]

    kern = functools.partial(_layer_kernel, layer=layer, apply_final_norm=(layer == depth - 1))
    return pl.pallas_call(
        kern,
        grid=(bsz // ns, seq // tl),
        in_specs=in_specs,
        out_specs=pl.BlockSpec((ns, tl, d), lambda b, l: (b, l, 0)),
        out_shape=jax.ShapeDtypeStruct(x.shape, x.dtype),
        scratch_shapes=[
            pltpu.VMEM((ns, HG_HEADS, HG_DV, HG_DK), F32),
            pltpu.VMEM((ns, M2_GROUPS, M2_STATE, M2_GW), F32),
            pltpu.VMEM((ns, CONV_PAD + tl, M2_CONV_DIM), F32),
            pltpu.VMEM((ns, tl, HG_W), BF16),
            pltpu.VMEM((D_MODEL, MAIN_COLS), BF16),
            pltpu.VMEM((HG_W, D_MODEL), BF16),
            pltpu.VMEM(w_out.shape, BF16),
            pltpu.VMEM((2, W_IN_STAGE_ROWS, D_MODEL), F32),
            pltpu.VMEM((2, W_OUT_STAGE_ROWS, w_out.shape[1]), F32),
            pltpu.SemaphoreType.DMA((2,)),
            pltpu.SemaphoreType.DMA((2,)),
        ],
        compiler_params=pltpu.CompilerParams(
            dimension_semantics=("arbitrary", "arbitrary"),
            vmem_limit_bytes=VMEM_LIMIT_BYTES,
        ),
        name="hybrid_layer",
    )(*operands)


@jax.jit
def kernel(x, norm_w, w_in, hg_lb_logits, hg_norm_w, m2_conv_w, m2_conv_b, m2_dt_bias, m2_a_log,
           m2_d_skip, m2_norm_w, w_out, final_norm_w):
    depth = w_in.shape[0]
    for l in range(depth):
        x = _layer(x, l, depth, norm_w[l], w_in[l], hg_lb_logits, hg_norm_w[l], m2_conv_w[l],
                   m2_conv_b[l], m2_dt_bias[l], m2_a_log[l], m2_d_skip[l], m2_norm_w[l], w_out[l],
                   final_norm_w)
    return x
```

```python
import functools
import math

import jax
import jax.numpy as jnp
from jax import lax
from jax.experimental import pallas as pl
from jax.experimental.pallas import tpu as pltpu

F32 = jnp.float32
BF16 = jnp.bfloat16

EPS = 1e-6
LOG2E = math.log2(math.e)
D_MODEL = 1024
HG_HEADS = 8
HG_DK = 128
HG_DV = 128
HG_W = HG_HEADS * HG_DK
HG_CHUNK = 128
M2_HEADDIM = 64
M2_W = 1024
M2_HEADS = M2_W // M2_HEADDIM
M2_GROUPS = 2
M2_STATE = 128
M2_CONV = 4
M2_CHUNK = 128
M2_CONV_DIM = M2_W + 2 * M2_GROUPS * M2_STATE
M2_GW = M2_W // M2_GROUPS
M2_HPG = M2_HEADS // M2_GROUPS
M2_QUAD = 4

OFF_Q = 0
OFF_F = OFF_Q + HG_W
OFF_I = OFF_F + HG_W
OFF_G = OFF_I + HG_W
OFF_Z = OFF_G + HG_W
OFF_XBC = OFF_Z + M2_W
OFF_DT = OFF_XBC + M2_CONV_DIM
MAIN_COLS = OFF_DT

SEQ_TILE = 2 * HG_CHUNK
STREAMS = 1
CONV_PAD = 8
W_IN_STAGE_ROWS = 128
W_OUT_STAGE_ROWS = 256
VMEM_LIMIT_BYTES = 58 * 1024 * 1024
HG_SAFE_LOG2_DECAY = -108.0


def _dot(a, b):
    return jnp.dot(a, b, preferred_element_type=F32)


def _dot_nt(a, b):
    return lax.dot_general(a, b, (((1,), (1,)), ((), ())), preferred_element_type=F32)


def _dot_tn(a, b):
    return lax.dot_general(a, b, (((0,), (0,)), ((), ())), preferred_element_type=F32)


def _split3(x):
    hi = x.astype(BF16)
    r1 = x - hi.astype(F32)
    mid = r1.astype(BF16)
    lo = (r1 - mid.astype(F32)).astype(BF16)
    return hi, mid, lo


def _dot_exact_left(mat_bf16, x):
    hi, mid, lo = _split3(x)
    return (_dot(mat_bf16, lo) + _dot(mat_bf16, mid)) + _dot(mat_bf16, hi)


def _dot_exact_right(x, mat_bf16):
    hi, mid, lo = _split3(x)
    return (_dot(lo, mat_bf16) + _dot(mid, mat_bf16)) + _dot(hi, mat_bf16)


def _sigmoid(x):
    return 1.0 / (1.0 + jnp.exp2(x * (-LOG2E)))


def _silu(x):
    return x * _sigmoid(x)


def _softplus(x):
    return jnp.maximum(x, 0.0) + jnp.log(1.0 + jnp.exp(-jnp.abs(x)))


def _iota2(shape):
    return lax.broadcasted_iota(jnp.int32, shape, 0), lax.broadcasted_iota(jnp.int32, shape, 1)


def _block_tril(n, block, upper=False):
    r, c = _iota2((n, n))
    same = (r // block) == (c // block)
    tri = (r <= c) if upper else (c <= r)
    return jnp.where(same & tri, 1.0, 0.0).astype(BF16)


def _block_diag2(a, b):
    z = jnp.zeros_like(a)
    return jnp.concatenate([jnp.concatenate([a, z], axis=1), jnp.concatenate([z, b], axis=1)], axis=0)


def _chunk_cumsum(x, chunk):
    n, w = x.shape
    sub = lax.broadcasted_iota(jnp.int32, (8, w), 0)
    out = []
    run = None
    for g in range(n // 8):
        blk = x[8 * g:8 * g + 8, :]
        for d in (1, 2, 4):
            blk = blk + jnp.where(sub >= d, pltpu.roll(blk, d, axis=0), 0.0)
        if (8 * g) % chunk != 0:
            blk = blk + run
        run = jnp.broadcast_to(blk[7:8, :], (8, w))
        out.append(blk)
    return jnp.concatenate(out, axis=0)


def _rms_rows(o):
    return o * lax.rsqrt(jnp.mean(o * o, axis=-1, keepdims=True) + EPS)


def _hg_same_chunk_exact(q, k, b2, tl):
    r, c = _iota2((tl, tl))
    q_bf = q.astype(BF16)
    k_bf = k.astype(BF16)
    atts = [jnp.where(r == c, _dot_nt(q_bf[:, hd * HG_DK:(hd + 1) * HG_DK], k_bf[:, hd * HG_DK:(hd + 1) * HG_DK]), 0.0)
            for hd in range(HG_HEADS)]
    row = lax.broadcasted_iota(jnp.int32, (tl, 1), 0)
    h = HG_CHUNK // 2
    while h >= 1:
        mid = (r // (2 * h)) * (2 * h) + (h - 1)
        sel = jnp.where(c == mid, 1.0, 0.0).astype(BF16)
        ref = _dot_exact_left(sel, b2)
        e = jnp.exp2(-jnp.abs(b2 - ref))
        is_q = (row % (2 * h)) >= h
        zq = jnp.where(is_q, q * e, 0.0).astype(BF16)
        zk = jnp.where(is_q, 0.0, k * e).astype(BF16)
        pair = ((r // (2 * h)) == (c // (2 * h))) & ((r % (2 * h)) >= h) & ((c % (2 * h)) < h)
        for hd in range(HG_HEADS):
            cols = slice(hd * HG_DK, (hd + 1) * HG_DK)
            atts[hd] = atts[hd] + jnp.where(pair, _dot_nt(zq[:, cols], zk[:, cols]), 0.0)
        h //= 2
    return atts


class _Vals(dict):
    __getattr__ = dict.__getitem__
    __setattr__ = dict.__setitem__


def _front(x, p, st, layer):
    tl = x.shape[0]
    v = _Vals(x=x)
    ms = jnp.mean(x * x, axis=-1, keepdims=True)
    h = (x * lax.rsqrt(ms + EPS)) * p.norm_w[...]
    h_bf = h.astype(BF16)
    h_lo = (h - h_bf.astype(F32)).astype(BF16)

    def proj(off, width):
        return _dot_nt(h_bf, p.w_main[off:off + width, :])

    fl = proj(OFF_F, HG_W)
    wdtT_hi = p.wdtT_hi[...]
    wdtT_lo = p.wdtT_lo[...]
    dt_rawT = (_dot_nt(wdtT_hi, h_lo) + _dot_nt(wdtT_lo, h_bf)) + _dot_nt(wdtT_hi, h_bf)
    q_raw = proj(OFF_Q, HG_W)
    v.dtT = _softplus(dt_rawT + p.dt_bias_col[...])
    aT = v.dtT * (jnp.exp(p.a_log_col[...]) * (-LOG2E))
    u = proj(OFF_XBC, M2_CONV_DIM)
    v.csT = _dot_exact_right(aT, _block_tril(tl, M2_CHUNK, upper=True))
    v.vT_bf = _dot_nt(p.w_main[OFF_I:OFF_I + HG_W, :], h_bf).astype(BF16)
    v.gate_hg = _silu(proj(OFF_G, HG_W)).astype(BF16)
    v.gate_m2 = _silu(proj(OFF_Z, M2_W)).astype(BF16)

    logits = p.lb_logits[...]
    mx = jnp.max(logits, axis=0, keepdims=True)
    ez = jnp.exp(logits - mx)
    lb = jnp.sum(ez[0:layer + 1], axis=0, keepdims=True) / jnp.sum(ez, axis=0, keepdims=True)
    one_m_lb = 1.0 - lb
    sg = _sigmoid(fl)
    f = lb + one_m_lb * sg
    k = one_m_lb * (1.0 - sg)
    b2 = _chunk_cumsum(jnp.log2(f), HG_CHUNK)
    e = jnp.exp2(b2)
    q = _silu(q_raw)
    qt = q * e
    v.q, v.k, v.b2 = q, k, b2
    v.qt_bf = qt.astype(BF16)
    v.kt_bf = (k / e).astype(BF16)
    c0, c1 = _C0, _C1
    tot0 = b2[HG_CHUNK - 1:HG_CHUNK, :]
    tot1 = b2[2 * HG_CHUNK - 1:2 * HG_CHUNK, :]
    khat0 = k[c0, :] * jnp.exp2(tot0 - b2[c0, :])
    khat1 = k[c1, :] * jnp.exp2(tot1 - b2[c1, :])
    v.khat0_bf = khat0.astype(BF16)
    v.kend_bf = jnp.concatenate([(khat0 * jnp.exp2(tot1)).astype(BF16), khat1.astype(BF16)], axis=0)
    v.qg_bf = jnp.concatenate([v.qt_bf[c0, :], (qt[c1, :] * jnp.exp2(tot0)).astype(BF16)], axis=0)
    v.keys1_bf = jnp.concatenate([v.khat0_bf, v.kt_bf[c1, :]], axis=0)
    v.e_tile = jnp.exp2(tot0 + tot1)
    v.same_chunk_safe = jnp.min(jnp.minimum(tot0, tot1)) >= HG_SAFE_LOG2_DECAY

    st.ubuf[CONV_PAD:CONV_PAD + tl, :] = u
    conv = u * p.conv_w[M2_CONV - 1:M2_CONV, :]
    for j in range(M2_CONV - 1):
        sh = M2_CONV - 1 - j
        conv = conv + st.ubuf[CONV_PAD - sh:CONV_PAD - sh + tl, :] * p.conv_w[j:j + 1, :]
    st.ubuf[0:CONV_PAD, :] = st.ubuf[tl:tl + CONV_PAD, :]
    xbc = _silu(conv + p.conv_b[...])
    v.xs = xbc[:, 0:M2_W]
    v.bm_bf = xbc[:, M2_W:M2_W + M2_GROUPS * M2_STATE].astype(BF16)
    v.cm_bf = xbc[:, M2_W + M2_GROUPS * M2_STATE:].astype(BF16)
    v.dt = v.dtT.T
    v.cs = v.csT.T
    return v


def _head_cols(hd):
    return slice(hd * HG_DK, (hd + 1) * HG_DK)


def _store_mix_hg(v, p, st, hd, o0, o1):
    cols = _head_cols(hd)
    for rows, o in ((_C0, o0), (_C1, o1)):
        st.mix_hg[rows, cols] = (_rms_rows(o) * p.hg_norm_w[:, cols] * v.gate_hg[rows, cols]).astype(BF16)


_C0 = slice(0, HG_CHUNK)
_C1 = slice(HG_CHUNK, 2 * HG_CHUNK)


def _waves(v, p, st):
    tl = v.x.shape[0]
    c0, c1 = _C0, _C1
    rc, cc = _iota2((HG_CHUNK, HG_CHUNK))
    causal0 = cc <= rc
    r1, k1 = _iota2((HG_CHUNK, tl))
    causal1 = k1 - HG_CHUNK <= r1
    v.o_inter = []
    for pr in range(HG_HEADS // 2):
        st_pair = _block_diag2(st.hg_state[2 * pr].astype(BF16), st.hg_state[2 * pr + 1].astype(BF16))
        oi = _dot_nt(v.qg_bf[:, 2 * pr * HG_DK:(2 * pr + 2) * HG_DK], st_pair)
        v.o_inter += [oi[:, 0:HG_DV], oi[:, HG_DV:2 * HG_DV]]
    att0, att1 = [], []
    for hd in range(HG_HEADS):
        cols = _head_cols(hd)
        att0.append(jnp.where(causal0, _dot_nt(v.qt_bf[c0, cols], v.kt_bf[c0, cols]), 0.0).astype(BF16))
        att1.append(jnp.where(causal1, _dot_nt(v.qt_bf[c1, cols], v.keys1_bf[:, cols]), 0.0).astype(BF16))
    for hd in range(HG_HEADS):
        cols = _head_cols(hd)
        st.hg_state[hd] = st.hg_state[hd] * v.e_tile[:, cols] + _dot(v.vT_bf[cols, :], v.kend_bf[:, cols])

    hr, hc = _iota2((M2_HEADS, M2_W))
    expand = jnp.where(hc // M2_HEADDIM == hr, 1.0, 0.0).astype(BF16)
    r2, c2 = _iota2((M2_CHUNK, M2_CHUNK))
    causal_m2 = c2 <= r2
    qr, qc = _iota2((M2_QUAD * M2_CHUNK, M2_QUAD * M2_HEADDIM))
    quad_mask = (qr // M2_CHUNK) == (qc // M2_HEADDIM)
    nck = tl // M2_CHUNK
    rows_of = lambda c: slice(c * M2_CHUNK, (c + 1) * M2_CHUNK)
    ncols_of = lambda gi: slice(gi * M2_STATE, (gi + 1) * M2_STATE)
    gcols_of = lambda gi: slice(gi * M2_GW, (gi + 1) * M2_GW)
    xs, cs, csT, dtT = v.xs, v.cs, v.csT, v.dtT
    xs_bf = xs.astype(BF16)
    cb = {(c, gi): _dot_nt(v.cm_bf[rows_of(c), ncols_of(gi)], v.bm_bf[rows_of(c), ncols_of(gi)])
          for c in range(nck) for gi in range(M2_GROUPS)}
    ecs_x, xdec_bf = [], []
    for c in range(nck):
        cs_c = cs[rows_of(c), :]
        cs_last = cs_c[M2_CHUNK - 1:M2_CHUNK, :]
        ecs_x.append(_dot(jnp.exp2(cs_c).astype(BF16), expand))
        dec_x = _dot((jnp.exp2(cs_last - cs_c) * v.dt[rows_of(c), :]).astype(BF16), expand)
        xdec_bf.append((xs[rows_of(c), :] * dec_x).astype(BF16))
    y_off = {}
    for gi in range(M2_GROUPS):
        stT = st.m2_state[gi]
        for c in range(nck):
            cg = v.cm_bf[rows_of(c), ncols_of(gi)]
            y_off[c, gi] = _dot(cg, stT.astype(BF16)) * ecs_x[c][:, gcols_of(gi)]
            e_last_x = ecs_x[c][M2_CHUNK - 1:M2_CHUNK, gcols_of(gi)]
            stT = stT * e_last_x + _dot_tn(v.bm_bf[rows_of(c), ncols_of(gi)], xdec_bf[c][:, gcols_of(gi)])
        st.m2_state[gi] = stT

    for hd in range(HG_HEADS):
        cols = _head_cols(hd)
        vT = v.vT_bf[cols, :]
        _store_mix_hg(v, p, st, hd, v.o_inter[hd][c0, :] + _dot_nt(att0[hd], vT[:, c0]),
                      v.o_inter[hd][c1, :] + _dot_nt(att1[hd], vT))

    y_rows = []
    for c in range(nck):
        rows = rows_of(c)
        y_quads = []
        for gi in range(M2_GROUPS):
            for qd in range(M2_HPG // M2_QUAD):
                ws = []
                for hh in range(M2_QUAD):
                    hidx = gi * M2_HPG + qd * M2_QUAD + hh
                    seg = cs[rows, hidx:hidx + 1] - csT[hidx:hidx + 1, rows]
                    lm = jnp.exp2(jnp.where(causal_m2, seg, -jnp.inf))
                    ws.append(((cb[c, gi] * lm) * dtT[hidx:hidx + 1, rows]).astype(BF16))
                q0 = (gi * M2_HPG + qd * M2_QUAD) * M2_HEADDIM
                xq = xs_bf[rows, q0:q0 + M2_QUAD * M2_HEADDIM]
                x_bd = jnp.where(quad_mask, jnp.concatenate([xq] * M2_QUAD, axis=0), jnp.zeros_like(quad_mask, BF16))
                y_diag = _dot(jnp.concatenate(ws, axis=1), x_bd)
                o0 = qd * M2_QUAD * M2_HEADDIM
                y_quads.append(y_diag + y_off[c, gi][:, o0:o0 + M2_QUAD * M2_HEADDIM])
        y_rows.append(jnp.concatenate(y_quads, axis=1))

    y = jnp.concatenate(y_rows, axis=0) + p.d_skip[...] * xs
    y = y * v.gate_m2
    mix_m2 = (jnp.concatenate([_rms_rows(y[:, gi * M2_GW:(gi + 1) * M2_GW]) for gi in range(M2_GROUPS)], axis=-1)
              * p.m2_norm_w[...]).astype(BF16)
    v.out_m2 = _dot(mix_m2, p.w_out[HG_W:HG_W + M2_W, :])


def _redo_heads_exact(v, p, st):
    c0, c1 = _C0, _C1
    same = _hg_same_chunk_exact(v.q, v.k, v.b2, v.x.shape[0])
    for hd in range(HG_HEADS):
        cols = _head_cols(hd)
        vT = v.vT_bf[cols, :]
        cross = _dot_nt(v.qt_bf[c1, cols], v.khat0_bf[:, cols])
        o0 = v.o_inter[hd][c0, :] + _dot_nt(same[hd][c0, c0].astype(BF16), vT[:, c0])
        o1 = (v.o_inter[hd][c1, :] + _dot_nt(cross.astype(BF16), vT[:, c0])
              + _dot_nt(same[hd][c1, c1].astype(BF16), vT[:, c1]))
        _store_mix_hg(v, p, st, hd, o0, o1)


def _tail(v, p, st, apply_final_norm):
    xr = v.x + (_dot(st.mix_hg[...], p.w_out[0:HG_W, :]) + v.out_m2)
    if apply_final_norm:
        ms2 = jnp.mean(xr * xr, axis=-1, keepdims=True)
        xr = (xr * lax.rsqrt(ms2 + EPS)) * p.final_norm_w[...]
    return xr


def _stream_rows(src_hbm, nrows, stage_ref, sem, consume):
    rc = stage_ref.shape[1]
    nchunk = nrows // rc
    assert nchunk * rc == nrows

    def chunk_copy(i):
        return pltpu.make_async_copy(src_hbm.at[pl.ds(i * rc, rc), :], stage_ref.at[i % 2], sem.at[i % 2])

    chunk_copy(0).start()
    for i in range(nchunk):
        if i + 1 < nchunk:
            chunk_copy(i + 1).start()
        chunk_copy(i).wait()
        consume(i, stage_ref[i % 2])


def _load_weights(w_inT_hbm, w_out_hbm, w_main_ref, w_out_ref, stage_in_ref, stage_out_ref, sem_in, sem_out):
    rc = stage_in_ref.shape[1]

    def put_in(i, chunk):
        w_main_ref[i * rc:(i + 1) * rc, :] = chunk.astype(BF16)

    def put_out(i, chunk):
        ro = stage_out_ref.shape[1]
        w_out_ref[i * ro:(i + 1) * ro, :] = chunk.astype(BF16)

    _stream_rows(w_inT_hbm, MAIN_COLS, stage_in_ref, sem_in, put_in)
    _stream_rows(w_out_hbm, w_out_hbm.shape[0], stage_out_ref, sem_out, put_out)


def _layer_kernel(x_ref, norm_w_ref, w_inT_hbm, wdtT_hi_ref, wdtT_lo_ref,
                  lb_logits_ref, hg_norm_w_ref, conv_w_ref, conv_b_ref,
                  dt_bias_col_ref, a_log_col_ref,
                  d_skip_ref, m2_norm_w_ref, w_out_hbm, final_norm_w_ref,
                  o_ref,
                  hg_state_ref, m2_state_ref, ubuf_ref, mix_hg_ref,
                  w_main_ref, w_out_ref, stage_in_ref, stage_out_ref, sem_in, sem_out,
                  *, layer, apply_final_norm):
    nstream, tl = x_ref.shape[0], x_ref.shape[1]
    assert tl == 2 * HG_CHUNK and tl % M2_CHUNK == 0
    p = _Vals(norm_w=norm_w_ref, w_main=w_main_ref, wdtT_hi=wdtT_hi_ref, wdtT_lo=wdtT_lo_ref,
              lb_logits=lb_logits_ref, hg_norm_w=hg_norm_w_ref, conv_w=conv_w_ref, conv_b=conv_b_ref,
              dt_bias_col=dt_bias_col_ref, a_log_col=a_log_col_ref, d_skip=d_skip_ref,
              m2_norm_w=m2_norm_w_ref, w_out=w_out_ref, final_norm_w=final_norm_w_ref)
    sts = [_Vals(hg_state=hg_state_ref.at[s], m2_state=m2_state_ref.at[s], ubuf=ubuf_ref.at[s],
                 mix_hg=mix_hg_ref.at[s]) for s in range(nstream)]

    @pl.when((pl.program_id(0) == 0) & (pl.program_id(1) == 0))
    def _():
        _load_weights(w_inT_hbm, w_out_hbm, w_main_ref, w_out_ref,
                      stage_in_ref, stage_out_ref, sem_in, sem_out)

    @pl.when(pl.program_id(1) == 0)
    def _():
        hg_state_ref[...] = jnp.zeros_like(hg_state_ref)
        m2_state_ref[...] = jnp.zeros_like(m2_state_ref)
        for s in range(nstream):
            ubuf_ref[s, 0:CONV_PAD, :] = jnp.zeros((CONV_PAD, M2_CONV_DIM), F32)

    vals = []
    for s in range(nstream):
        vals.append(_front(x_ref[s], p, sts[s], layer))
        _waves(vals[s], p, sts[s])

    for s in range(nstream):
        @pl.when(jnp.logical_not(vals[s].same_chunk_safe))
        def _(s=s):
            _redo_heads_exact(vals[s], p, sts[s])

    for s in range(nstream):
        o_ref[s] = _tail(vals[s], p, sts[s], apply_final_norm).astype(o_ref.dtype)


def _const_spec(shape):
    nd = len(shape)
    return pl.BlockSpec(shape, lambda b, l: (0,) * nd, pipeline_mode=pl.Buffered(1))


def _layer(x, layer, depth, norm_w, w_in, hg_lb_logits, hg_norm_w, conv_w, conv_b, dt_bias, a_log,
           d_skip, m2_norm_w, w_out, final_norm_w):
    bsz, seq, d = x.shape
    assert d == D_MODEL and seq % SEQ_TILE == 0 and bsz % STREAMS == 0
    assert w_in.shape == (D_MODEL, MAIN_COLS + M2_HEADS)
    tl, ns = SEQ_TILE, STREAMS

    wdt = w_in[:, MAIN_COLS:]
    wdt_hi = wdt.astype(BF16)
    wdt_lo = (wdt - wdt_hi.astype(F32)).astype(BF16)
    row = lambda t: t.reshape(1, -1).astype(F32)
    col = lambda t: t.reshape(-1, 1).astype(F32)

    operands = (
        x, row(norm_w), w_in.astype(F32).T, wdt_hi.T, wdt_lo.T,
        hg_lb_logits.astype(F32), row(hg_norm_w), conv_w.astype(F32), row(conv_b),
        col(dt_bias), col(a_log),
        row(jnp.repeat(d_skip, M2_HEADDIM)), row(m2_norm_w), w_out.astype(F32), row(final_norm_w),
    )
    hbm_operands = (2, 13)
    in_specs = [pl.BlockSpec((ns, tl, d), lambda b, l: (b, l, 0))]
    in_specs += [pl.BlockSpec(memory_space=pl.ANY) if i in hbm_operands else _const_spec(op.shape)
                 for i, op in enumerate(operands) if i > 0]

    kern = functools.partial(_layer_kernel, layer=layer, apply_final_norm=(layer == depth - 1))
    return pl.pallas_call(
        kern,
        grid=(bsz // ns, seq // tl),
        in_specs=in_specs,
        out_specs=pl.BlockSpec((ns, tl, d), lambda b, l: (b, l, 0)),
        out_shape=jax.ShapeDtypeStruct(x.shape, x.dtype),
        scratch_shapes=[
            pltpu.VMEM((ns, HG_HEADS, HG_DV, HG_DK), F32),
            pltpu.VMEM((ns, M2_GROUPS, M2_STATE, M2_GW), F32),
            pltpu.VMEM((ns, CONV_PAD + tl, M2_CONV_DIM), F32),
            pltpu.VMEM((ns, tl, HG_W), BF16),
            pltpu.VMEM((MAIN_COLS, D_MODEL), BF16),
            pltpu.VMEM(w_out.shape, BF16),
            pltpu.VMEM((2, W_IN_STAGE_ROWS, D_MODEL), F32),
            pltpu.VMEM((2, W_OUT_STAGE_ROWS, w_out.shape[1]), F32),
            pltpu.SemaphoreType.DMA((2,)),
            pltpu.SemaphoreType.DMA((2,)),
        ],
        compiler_params=pltpu.CompilerParams(
            dimension_semantics=("arbitrary", "arbitrary"),
            vmem_limit_bytes=VMEM_LIMIT_BYTES,
        ),
        name="hybrid_layer",
    )(*operands)


@jax.jit
def kernel(x, norm_w, w_in, hg_lb_logits, hg_norm_w, m2_conv_w, m2_conv_b, m2_dt_bias, m2_a_log,
           m2_d_skip, m2_norm_w, w_out, final_norm_w):
    depth = w_in.shape[0]
    for l in range(depth):
        x = _layer(x, l, depth, norm_w[l], w_in[l], hg_lb_logits, hg_norm_w[l], m2_conv_w[l],
                   m2_conv_b[l], m2_dt_bias[l], m2_a_log[l], m2_d_skip[l], m2_norm_w[l], w_out[l],
                   final_norm_w)
    return x
```

```python
import functools
import math

import jax
import jax.numpy as jnp
from jax import lax
from jax.experimental import pallas as pl
from jax.experimental.pallas import tpu as pltpu

F32 = jnp.float32
BF16 = jnp.bfloat16

EPS = 1e-6
LOG2E = math.log2(math.e)
D_MODEL = 1024
HG_HEADS = 8
HG_DK = 128
HG_DV = 128
HG_W = HG_HEADS * HG_DK
HG_CHUNK = 128
M2_HEADDIM = 64
M2_W = 1024
M2_HEADS = M2_W // M2_HEADDIM
M2_GROUPS = 2
M2_STATE = 128
M2_CONV = 4
M2_CHUNK = 128
M2_CONV_DIM = M2_W + 2 * M2_GROUPS * M2_STATE
M2_GW = M2_W // M2_GROUPS
M2_HPG = M2_HEADS // M2_GROUPS
M2_QUAD = 4

OFF_Q = 0
OFF_F = OFF_Q + HG_W
OFF_I = OFF_F + HG_W
OFF_G = OFF_I + HG_W
OFF_Z = OFF_G + HG_W
OFF_XBC = OFF_Z + M2_W
OFF_DT = OFF_XBC + M2_CONV_DIM
MAIN_COLS = OFF_DT

SEQ_TILE = 2 * HG_CHUNK
STREAMS = 1
CONV_PAD = 8
W_IN_STAGE_ROWS = 128
W_OUT_STAGE_ROWS = 256
W_STAGE_SLOTS = 4
VMEM_LIMIT_BYTES = 58 * 1024 * 1024
HG_SAFE_LOG2_DECAY = -108.0


def _dot(a, b):
    return jnp.dot(a, b, preferred_element_type=F32)


def _dot_nt(a, b):
    return lax.dot_general(a, b, (((1,), (1,)), ((), ())), preferred_element_type=F32)


def _dot_tn(a, b):
    return lax.dot_general(a, b, (((0,), (0,)), ((), ())), preferred_element_type=F32)


def _split3(x):
    hi = x.astype(BF16)
    r1 = x - hi.astype(F32)
    mid = r1.astype(BF16)
    lo = (r1 - mid.astype(F32)).astype(BF16)
    return hi, mid, lo


def _dot_exact_left(mat_bf16, x):
    hi, mid, lo = _split3(x)
    return (_dot(mat_bf16, lo) + _dot(mat_bf16, mid)) + _dot(mat_bf16, hi)


def _dot_exact_right(x, mat_bf16):
    hi, mid, lo = _split3(x)
    return (_dot(lo, mat_bf16) + _dot(mid, mat_bf16)) + _dot(hi, mat_bf16)


def _sigmoid(x):
    return 1.0 / (1.0 + jnp.exp2(x * (-LOG2E)))


def _silu(x):
    return x * _sigmoid(x)


def _softplus(x):
    return jnp.maximum(x, 0.0) + jnp.log(1.0 + jnp.exp(-jnp.abs(x)))


def _iota2(shape):
    return lax.broadcasted_iota(jnp.int32, shape, 0), lax.broadcasted_iota(jnp.int32, shape, 1)


def _block_tril(n, block, upper=False):
    r, c = _iota2((n, n))
    same = (r // block) == (c // block)
    tri = (r <= c) if upper else (c <= r)
    return jnp.where(same & tri, 1.0, 0.0).astype(BF16)


def _block_diag2(a, b):
    z = jnp.zeros_like(a)
    return jnp.concatenate([jnp.concatenate([a, z], axis=1), jnp.concatenate([z, b], axis=1)], axis=0)


def _chunk_cumsum(x, chunk):
    n, w = x.shape
    sub = lax.broadcasted_iota(jnp.int32, (8, w), 0)
    out = []
    run = None
    for g in range(n // 8):
        blk = x[8 * g:8 * g + 8, :]
        for d in (1, 2, 4):
            blk = blk + jnp.where(sub >= d, pltpu.roll(blk, d, axis=0), 0.0)
        if (8 * g) % chunk != 0:
            blk = blk + run
        run = jnp.broadcast_to(blk[7:8, :], (8, w))
        out.append(blk)
    return jnp.concatenate(out, axis=0)


def _rms_rows(o):
    return o * lax.rsqrt(jnp.mean(o * o, axis=-1, keepdims=True) + EPS)


def _hg_same_chunk_exact(q, k, b2, tl):
    r, c = _iota2((tl, tl))
    q_bf = q.astype(BF16)
    k_bf = k.astype(BF16)
    atts = [jnp.where(r == c, _dot_nt(q_bf[:, hd * HG_DK:(hd + 1) * HG_DK], k_bf[:, hd * HG_DK:(hd + 1) * HG_DK]), 0.0)
            for hd in range(HG_HEADS)]
    row = lax.broadcasted_iota(jnp.int32, (tl, 1), 0)
    h = HG_CHUNK // 2
    while h >= 1:
        mid = (r // (2 * h)) * (2 * h) + (h - 1)
        sel = jnp.where(c == mid, 1.0, 0.0).astype(BF16)
        ref = _dot_exact_left(sel, b2)
        e = jnp.exp2(-jnp.abs(b2 - ref))
        is_q = (row % (2 * h)) >= h
        zq = jnp.where(is_q, q * e, 0.0).astype(BF16)
        zk = jnp.where(is_q, 0.0, k * e).astype(BF16)
        pair = ((r // (2 * h)) == (c // (2 * h))) & ((r % (2 * h)) >= h) & ((c % (2 * h)) < h)
        for hd in range(HG_HEADS):
            cols = slice(hd * HG_DK, (hd + 1) * HG_DK)
            atts[hd] = atts[hd] + jnp.where(pair, _dot_nt(zq[:, cols], zk[:, cols]), 0.0)
        h //= 2
    return atts


class _Vals(dict):
    __getattr__ = dict.__getitem__
    __setattr__ = dict.__setitem__


def _front(x, p, st, layer):
    tl = x.shape[0]
    v = _Vals(x=x)
    ms = jnp.mean(x * x, axis=-1, keepdims=True)
    h = (x * lax.rsqrt(ms + EPS)) * p.norm_w[...]
    h_bf = h.astype(BF16)
    h_lo = (h - h_bf.astype(F32)).astype(BF16)

    def proj(off, width):
        return _dot(h_bf, p.w_main[:, off:off + width])

    fl = proj(OFF_F, HG_W)
    wdtT_hi = p.wdtT_hi[...]
    wdtT_lo = p.wdtT_lo[...]
    dt_rawT = (_dot_nt(wdtT_hi, h_lo) + _dot_nt(wdtT_lo, h_bf)) + _dot_nt(wdtT_hi, h_bf)
    q_raw = proj(OFF_Q, HG_W)
    v.dtT = _softplus(dt_rawT + p.dt_bias_col[...])
    aT = v.dtT * (jnp.exp(p.a_log_col[...]) * (-LOG2E))
    u = proj(OFF_XBC, M2_CONV_DIM)
    v.csT = _dot_exact_right(aT, _block_tril(tl, M2_CHUNK, upper=True))
    v.vT_bf = _dot_nt(p.w_iT[...], h_bf).astype(BF16)
    v.gate_hg = _silu(proj(OFF_G, HG_W)).astype(BF16)
    v.gate_m2 = _silu(proj(OFF_Z, M2_W)).astype(BF16)

    logits = p.lb_logits[...]
    mx = jnp.max(logits, axis=0, keepdims=True)
    ez = jnp.exp(logits - mx)
    lb = jnp.sum(ez[0:layer + 1], axis=0, keepdims=True) / jnp.sum(ez, axis=0, keepdims=True)
    one_m_lb = 1.0 - lb
    sg = _sigmoid(fl)
    f = lb + one_m_lb * sg
    k = one_m_lb * (1.0 - sg)
    b2 = _chunk_cumsum(jnp.log2(f), HG_CHUNK)
    e = jnp.exp2(b2)
    q = _silu(q_raw)
    qt = q * e
    v.q, v.k, v.b2 = q, k, b2
    v.qt_bf = qt.astype(BF16)
    v.kt_bf = (k / e).astype(BF16)
    c0, c1 = _C0, _C1
    tot0 = b2[HG_CHUNK - 1:HG_CHUNK, :]
    tot1 = b2[2 * HG_CHUNK - 1:2 * HG_CHUNK, :]
    khat0 = k[c0, :] * jnp.exp2(tot0 - b2[c0, :])
    khat1 = k[c1, :] * jnp.exp2(tot1 - b2[c1, :])
    v.khat0_bf = khat0.astype(BF16)
    v.kend_bf = jnp.concatenate([(khat0 * jnp.exp2(tot1)).astype(BF16), khat1.astype(BF16)], axis=0)
    v.qg_bf = jnp.concatenate([v.qt_bf[c0, :], (qt[c1, :] * jnp.exp2(tot0)).astype(BF16)], axis=0)
    v.keys1_bf = jnp.concatenate([v.khat0_bf, v.kt_bf[c1, :]], axis=0)
    v.e_tile = jnp.exp2(tot0 + tot1)
    v.same_chunk_safe = jnp.min(jnp.minimum(tot0, tot1)) >= HG_SAFE_LOG2_DECAY

    st.ubuf[CONV_PAD:CONV_PAD + tl, :] = u
    conv = u * p.conv_w[M2_CONV - 1:M2_CONV, :]
    for j in range(M2_CONV - 1):
        sh = M2_CONV - 1 - j
        conv = conv + st.ubuf[CONV_PAD - sh:CONV_PAD - sh + tl, :] * p.conv_w[j:j + 1, :]
    st.ubuf[0:CONV_PAD, :] = st.ubuf[tl:tl + CONV_PAD, :]
    xbc = _silu(conv + p.conv_b[...])
    v.xs = xbc[:, 0:M2_W]
    v.bm_bf = xbc[:, M2_W:M2_W + M2_GROUPS * M2_STATE].astype(BF16)
    v.cm_bf = xbc[:, M2_W + M2_GROUPS * M2_STATE:].astype(BF16)
    v.dt = v.dtT.T
    v.cs = v.csT.T
    return v


def _head_cols(hd):
    return slice(hd * HG_DK, (hd + 1) * HG_DK)


def _store_mix_hg(v, p, st, hd, o0, o1):
    cols = _head_cols(hd)
    for rows, o in ((_C0, o0), (_C1, o1)):
        st.mix_hg[rows, cols] = (_rms_rows(o) * p.hg_norm_w[:, cols] * v.gate_hg[rows, cols]).astype(BF16)


_C0 = slice(0, HG_CHUNK)
_C1 = slice(HG_CHUNK, 2 * HG_CHUNK)


def _waves(v, p, st):
    tl = v.x.shape[0]
    c0, c1 = _C0, _C1
    rc, cc = _iota2((HG_CHUNK, HG_CHUNK))
    causal0 = cc <= rc
    r1, k1 = _iota2((HG_CHUNK, tl))
    causal1 = k1 - HG_CHUNK <= r1
    v.o_inter = []
    for pr in range(HG_HEADS // 2):
        st_pair = _block_diag2(st.hg_state[2 * pr].astype(BF16), st.hg_state[2 * pr + 1].astype(BF16))
        oi = _dot_nt(v.qg_bf[:, 2 * pr * HG_DK:(2 * pr + 2) * HG_DK], st_pair)
        v.o_inter += [oi[:, 0:HG_DV], oi[:, HG_DV:2 * HG_DV]]
    att0, att1 = [], []
    for hd in range(HG_HEADS):
        cols = _head_cols(hd)
        att0.append(jnp.where(causal0, _dot_nt(v.qt_bf[c0, cols], v.kt_bf[c0, cols]), 0.0).astype(BF16))
        att1.append(jnp.where(causal1, _dot_nt(v.qt_bf[c1, cols], v.keys1_bf[:, cols]), 0.0).astype(BF16))
    for hd in range(HG_HEADS):
        cols = _head_cols(hd)
        st.hg_state[hd] = st.hg_state[hd] * v.e_tile[:, cols] + _dot(v.vT_bf[cols, :], v.kend_bf[:, cols])

    hr, hc = _iota2((M2_HEADS, M2_W))
    expand = jnp.where(hc // M2_HEADDIM == hr, 1.0, 0.0).astype(BF16)
    r2, c2 = _iota2((M2_CHUNK, M2_CHUNK))
    causal_m2 = c2 <= r2
    qr, qc = _iota2((M2_QUAD * M2_CHUNK, M2_QUAD * M2_HEADDIM))
    quad_mask = (qr // M2_CHUNK) == (qc // M2_HEADDIM)
    nck = tl // M2_CHUNK
    rows_of = lambda c: slice(c * M2_CHUNK, (c + 1) * M2_CHUNK)
    ncols_of = lambda gi: slice(gi * M2_STATE, (gi + 1) * M2_STATE)
    gcols_of = lambda gi: slice(gi * M2_GW, (gi + 1) * M2_GW)
    xs, cs, csT, dtT = v.xs, v.cs, v.csT, v.dtT
    xs_bf = xs.astype(BF16)
    cb = {(c, gi): _dot_nt(v.cm_bf[rows_of(c), ncols_of(gi)], v.bm_bf[rows_of(c), ncols_of(gi)])
          for c in range(nck) for gi in range(M2_GROUPS)}
    ecs_x, xdec_bf = [], []
    for c in range(nck):
        cs_c = cs[rows_of(c), :]
        cs_last = cs_c[M2_CHUNK - 1:M2_CHUNK, :]
        ecs_x.append(_dot(jnp.exp2(cs_c).astype(BF16), expand))
        dec_x = _dot((jnp.exp2(cs_last - cs_c) * v.dt[rows_of(c), :]).astype(BF16), expand)
        xdec_bf.append((xs[rows_of(c), :] * dec_x).astype(BF16))
    y_off = {}
    for gi in range(M2_GROUPS):
        stT = st.m2_state[gi]
        for c in range(nck):
            cg = v.cm_bf[rows_of(c), ncols_of(gi)]
            y_off[c, gi] = _dot(cg, stT.astype(BF16)) * ecs_x[c][:, gcols_of(gi)]
            e_last_x = ecs_x[c][M2_CHUNK - 1:M2_CHUNK, gcols_of(gi)]
            stT = stT * e_last_x + _dot_tn(v.bm_bf[rows_of(c), ncols_of(gi)], xdec_bf[c][:, gcols_of(gi)])
        st.m2_state[gi] = stT

    for hd in range(HG_HEADS):
        cols = _head_cols(hd)
        vT = v.vT_bf[cols, :]
        _store_mix_hg(v, p, st, hd, v.o_inter[hd][c0, :] + _dot_nt(att0[hd], vT[:, c0]),
                      v.o_inter[hd][c1, :] + _dot_nt(att1[hd], vT))

    y_rows = []
    for c in range(nck):
        rows = rows_of(c)
        y_quads = []
        for gi in range(M2_GROUPS):
            for qd in range(M2_HPG // M2_QUAD):
                ws = []
                for hh in range(M2_QUAD):
                    hidx = gi * M2_HPG + qd * M2_QUAD + hh
                    seg = cs[rows, hidx:hidx + 1] - csT[hidx:hidx + 1, rows]
                    lm = jnp.exp2(jnp.where(causal_m2, seg, -jnp.inf))
                    ws.append(((cb[c, gi] * lm) * dtT[hidx:hidx + 1, rows]).astype(BF16))
                q0 = (gi * M2_HPG + qd * M2_QUAD) * M2_HEADDIM
                xq = xs_bf[rows, q0:q0 + M2_QUAD * M2_HEADDIM]
                x_bd = jnp.where(quad_mask, jnp.concatenate([xq] * M2_QUAD, axis=0), jnp.zeros_like(quad_mask, BF16))
                y_diag = _dot(jnp.concatenate(ws, axis=1), x_bd)
                o0 = qd * M2_QUAD * M2_HEADDIM
                y_quads.append(y_diag + y_off[c, gi][:, o0:o0 + M2_QUAD * M2_HEADDIM])
        y_rows.append(jnp.concatenate(y_quads, axis=1))

    y = jnp.concatenate(y_rows, axis=0) + p.d_skip[...] * xs
    y = y * v.gate_m2
    mix_m2 = (jnp.concatenate([_rms_rows(y[:, gi * M2_GW:(gi + 1) * M2_GW]) for gi in range(M2_GROUPS)], axis=-1)
              * p.m2_norm_w[...]).astype(BF16)
    v.out_m2 = _dot(mix_m2, p.w_out[HG_W:HG_W + M2_W, :])


def _redo_heads_exact(v, p, st):
    c0, c1 = _C0, _C1
    same = _hg_same_chunk_exact(v.q, v.k, v.b2, v.x.shape[0])
    for hd in range(HG_HEADS):
        cols = _head_cols(hd)
        vT = v.vT_bf[cols, :]
        cross = _dot_nt(v.qt_bf[c1, cols], v.khat0_bf[:, cols])
        o0 = v.o_inter[hd][c0, :] + _dot_nt(same[hd][c0, c0].astype(BF16), vT[:, c0])
        o1 = (v.o_inter[hd][c1, :] + _dot_nt(cross.astype(BF16), vT[:, c0])
              + _dot_nt(same[hd][c1, c1].astype(BF16), vT[:, c1]))
        _store_mix_hg(v, p, st, hd, o0, o1)


def _tail(v, p, st, apply_final_norm):
    xr = v.x + (_dot(st.mix_hg[...], p.w_out[0:HG_W, :]) + v.out_m2)
    if apply_final_norm:
        ms2 = jnp.mean(xr * xr, axis=-1, keepdims=True)
        xr = (xr * lax.rsqrt(ms2 + EPS)) * p.final_norm_w[...]
    return xr


def _stream_rows(src_hbm, nrows, stage_ref, sem, consume):
    nslot, rc = stage_ref.shape[0], stage_ref.shape[1]
    nchunk = nrows // rc
    assert nchunk * rc == nrows and nslot >= 2

    def chunk_copy(i):
        return pltpu.make_async_copy(src_hbm.at[pl.ds(i * rc, rc), :], stage_ref.at[i % nslot], sem.at[i % nslot])

    for i in range(min(nslot - 1, nchunk)):
        chunk_copy(i).start(priority=i % 2)
    for i in range(nchunk):
        nxt = i + nslot - 1
        if nxt < nchunk:
            chunk_copy(nxt).start(priority=nxt % 2)
        chunk_copy(i).wait()
        consume(i, stage_ref[i % nslot])


def _load_weights(w_inT_hbm, w_out_hbm, w_main_ref, w_iT_ref, w_out_ref, stage_in_ref, stage_out_ref, sem_in, sem_out):
    rc = stage_in_ref.shape[1]

    def put_in(i, chunk):
        w_main_ref[:, i * rc:(i + 1) * rc] = chunk.T.astype(BF16)
        lo = i * rc - OFF_I
        if 0 <= lo < HG_W:
            w_iT_ref[lo:lo + rc, :] = chunk.astype(BF16)

    def put_out(i, chunk):
        ro = stage_out_ref.shape[1]
        w_out_ref[i * ro:(i + 1) * ro, :] = chunk.astype(BF16)

    _stream_rows(w_inT_hbm, MAIN_COLS, stage_in_ref, sem_in, put_in)
    _stream_rows(w_out_hbm, w_out_hbm.shape[0], stage_out_ref, sem_out, put_out)


def _layer_kernel(x_ref, norm_w_ref, w_inT_hbm, wdtT_hi_ref, wdtT_lo_ref,
                  lb_logits_ref, hg_norm_w_ref, conv_w_ref, conv_b_ref,
                  dt_bias_col_ref, a_log_col_ref,
                  d_skip_ref, m2_norm_w_ref, w_out_hbm, final_norm_w_ref,
                  o_ref,
                  hg_state_ref, m2_state_ref, ubuf_ref, mix_hg_ref,
                  w_main_ref, w_iT_ref, w_out_ref, stage_in_ref, stage_out_ref, sem_in, sem_out,
                  *, layer, apply_final_norm):
    nstream, tl = x_ref.shape[0], x_ref.shape[1]
    assert tl == 2 * HG_CHUNK and tl % M2_CHUNK == 0
    p = _Vals(norm_w=norm_w_ref, w_main=w_main_ref, w_iT=w_iT_ref, wdtT_hi=wdtT_hi_ref, wdtT_lo=wdtT_lo_ref,
              lb_logits=lb_logits_ref, hg_norm_w=hg_norm_w_ref, conv_w=conv_w_ref, conv_b=conv_b_ref,
              dt_bias_col=dt_bias_col_ref, a_log_col=a_log_col_ref, d_skip=d_skip_ref,
              m2_norm_w=m2_norm_w_ref, w_out=w_out_ref, final_norm_w=final_norm_w_ref)
    sts = [_Vals(hg_state=hg_state_ref.at[s], m2_state=m2_state_ref.at[s], ubuf=ubuf_ref.at[s],
                 mix_hg=mix_hg_ref.at[s]) for s in range(nstream)]

    @pl.when((pl.program_id(0) == 0) & (pl.program_id(1) == 0))
    def _():
        _load_weights(w_inT_hbm, w_out_hbm, w_main_ref, w_iT_ref, w_out_ref,
                      stage_in_ref, stage_out_ref, sem_in, sem_out)

    @pl.when(pl.program_id(1) == 0)
    def _():
        hg_state_ref[...] = jnp.zeros_like(hg_state_ref)
        m2_state_ref[...] = jnp.zeros_like(m2_state_ref)
        for s in range(nstream):
            ubuf_ref[s, 0:CONV_PAD, :] = jnp.zeros((CONV_PAD, M2_CONV_DIM), F32)

    vals = []
    for s in range(nstream):
        vals.append(_front(x_ref[s], p, sts[s], layer))
        _waves(vals[s], p, sts[s])

    for s in range(nstream):
        @pl.when(jnp.logical_not(vals[s].same_chunk_safe))
        def _(s=s):
            _redo_heads_exact(vals[s], p, sts[s])

    for s in range(nstream):
        o_ref[s] = _tail(vals[s], p, sts[s], apply_final_norm).astype(o_ref.dtype)


def _const_spec(shape):
    nd = len(shape)
    return pl.BlockSpec(shape, lambda b, l: (0,) * nd, pipeline_mode=pl.Buffered(1))


def _layer(x, layer, depth, norm_w, w_in, hg_lb_logits, hg_norm_w, conv_w, conv_b, dt_bias, a_log,
           d_skip, m2_norm_w, w_out, final_norm_w):
    bsz, seq, d = x.shape
    assert d == D_MODEL and seq % SEQ_TILE == 0 and bsz % STREAMS == 0
    assert w_in.shape == (D_MODEL, MAIN_COLS + M2_HEADS)
    tl, ns = SEQ_TILE, STREAMS

    wdt = w_in[:, MAIN_COLS:]
    wdt_hi = wdt.astype(BF16)
    wdt_lo = (wdt - wdt_hi.astype(F32)).astype(BF16)
    row = lambda t: t.reshape(1, -1).astype(F32)
    col = lambda t: t.reshape(-1, 1).astype(F32)

    operands = (
        x, row(norm_w), w_in.astype(F32).T, wdt_hi.T, wdt_lo.T,
        hg_lb_logits.astype(F32), row(hg_norm_w), conv_w.astype(F32), row(conv_b),
        col(dt_bias), col(a_log),
        row(jnp.repeat(d_skip, M2_HEADDIM)), row(m2_norm_w), w_out.astype(F32), row(final_norm_w),
    )
    hbm_operands = (2, 13)
    in_specs = [pl.BlockSpec((ns, tl, d), lambda b, l: (b, l, 0))]
    in_specs += [pl.BlockSpec(memory_space=pl.ANY) if i in hbm_operands else _const_spec(op.shape)
                 for i, op in enumerate(operands) if i > 0]

    kern = functools.partial(_layer_kernel, layer=layer, apply_final_norm=(layer == depth - 1))
    return pl.pallas_call(
        kern,
        grid=(bsz // ns, seq // tl),
        in_specs=in_specs,
        out_specs=pl.BlockSpec((ns, tl, d), lambda b, l: (b, l, 0)),
        out_shape=jax.ShapeDtypeStruct(x.shape, x.dtype),
        scratch_shapes=[
            pltpu.VMEM((ns, HG_HEADS, HG_DV, HG_DK), F32),
            pltpu.VMEM((ns, M2_GROUPS, M2_STATE, M2_GW), F32),
            pltpu.VMEM((ns, CONV_PAD + tl, M2_CONV_DIM), F32),
            pltpu.VMEM((ns, tl, HG_W), BF16),
            pltpu.VMEM((D_MODEL, MAIN_COLS), BF16),
            pltpu.VMEM((HG_W, D_MODEL), BF16),
            pltpu.VMEM(w_out.shape, BF16),
            pltpu.VMEM((W_STAGE_SLOTS, W_IN_STAGE_ROWS, D_MODEL), F32),
            pltpu.VMEM((W_STAGE_SLOTS, W_OUT_STAGE_ROWS, w_out.shape[1]), F32),
            pltpu.SemaphoreType.DMA((W_STAGE_SLOTS,)),
            pltpu.SemaphoreType.DMA((W_STAGE_SLOTS,)),
        ],
        compiler_params=pltpu.CompilerParams(
            dimension_semantics=("arbitrary", "arbitrary"),
            vmem_limit_bytes=VMEM_LIMIT_BYTES,
        ),
        name="hybrid_layer",
    )(*operands)


@jax.jit
def kernel(x, norm_w, w_in, hg_lb_logits, hg_norm_w, m2_conv_w, m2_conv_b, m2_dt_bias, m2_a_log,
           m2_d_skip, m2_norm_w, w_out, final_norm_w):
    depth = w_in.shape[0]
    for l in range(depth):
        x = _layer(x, l, depth, norm_w[l], w_in[l], hg_lb_logits, hg_norm_w[l], m2_conv_w[l],
                   m2_conv_b[l], m2_dt_bias[l], m2_a_log[l], m2_d_skip[l], m2_norm_w[l], w_out[l],
                   final_norm_w)
    return x
```
